```python
import jax, jax.numpy as jnp
from jax import lax
import numpy as np

D_MODEL = 4096
BATCH = 32
SEQ = 256
DEPTH = 1
DEC_BATCH = 2
DEC_SEQ = 1024
PAST_LEN = 256

GRID_W = 64
N_HEADS = 16
HEAD_DIM = 128
ATT_DIM = N_HEADS * HEAD_DIM
CONV_DIM = 2048
CONV_WIDTH = 3
WIN_ROWS = 8
WIN_COLS = 16
N_EXPERTS = 16
EXPERT_FF = 2048
CAPACITY_FACTOR = 2
N_MOD = 6
Q_BLOCK = 128
RMS_EPS = 1e-6
IN_DIM = 3 * ATT_DIM + 3 * CONV_DIM + 2 * D_MODEL
SPLITS = (ATT_DIM, 2 * ATT_DIM, 3 * ATT_DIM,
          3 * ATT_DIM + CONV_DIM, 3 * ATT_DIM + 2 * CONV_DIM, 3 * ATT_DIM + 3 * CONV_DIM,
          3 * ATT_DIM + 3 * CONV_DIM + D_MODEL)

kernel_name = "hybrid_natten_shortconv_ecmoe_diffusion_step"


def rmsnorm(x, g):
    xf = x.astype(jnp.float32)
    y = xf * lax.rsqrt(jnp.mean(xf * xf, axis=-1, keepdims=True) + RMS_EPS)
    return (y * g.astype(jnp.float32)).astype(x.dtype)


def adaln(cond, w_ada, b_ada):
    m = jax.nn.silu(cond) @ w_ada + b_ada
    return jnp.split(m[:, None, :], N_MOD, axis=-1)


def modulated_norm(x, g, shift, scale):
    return rmsnorm(x, g) * (1 + scale) + shift


def context_attention(q, k, v):
    b, l, h, hd = q.shape
    nb = l // Q_BLOCK
    qb = (q * HEAD_DIM ** -0.5).reshape(b, nb, Q_BLOCK, h, hd).transpose(1, 0, 2, 3, 4)

    def one_block(qblk):
        s = jnp.einsum('bqhd,bkhd->bhqk', qblk, k).astype(jnp.float32)
        p = jax.nn.softmax(s, axis=-1).astype(v.dtype)
        return jnp.einsum('bhqk,bkhd->bqhd', p, v)

    o = lax.map(one_block, qb)
    return o.transpose(1, 0, 2, 3, 4).reshape(b, l, h * hd)


def neighborhood_attention(q, k, v, k_ctx, v_ctx, rpb):
    b, t, h, hd = q.shape
    rows = t // GRID_W
    kr = min(WIN_ROWS, rows)
    q_rows = jnp.arange(rows)
    r0 = jnp.clip(q_rows - kr // 2, 0, rows - kr)
    row_idx = r0[:, None] + jnp.arange(kr)[None, :]
    qg = (q * HEAD_DIM ** -0.5).reshape(b, rows, GRID_W, h, hd)
    kg = k.reshape(b, rows, GRID_W, h, hd)[:, row_idx]
    vg = v.reshape(b, rows, GRID_W, h, hd)[:, row_idx].reshape(b, rows, kr * GRID_W, h, hd)
    cols = jnp.arange(GRID_W)
    c0 = jnp.clip(cols - WIN_COLS // 2, 0, GRID_W - WIN_COLS)
    in_win = (cols[None, :] >= c0[:, None]) & (cols[None, :] < c0[:, None] + WIN_COLS)
    dr_idx = row_idx - q_rows[:, None] + WIN_ROWS - 1
    dc_idx = jnp.clip(cols[None, :] - cols[:, None] + WIN_COLS - 1, 0, 2 * WIN_COLS - 2)
    bias = rpb[:, dr_idx[:, None, :, None], dc_idx[None, :, None, :]]
    s_loc = jnp.einsum('brqhd,brkchd->bhrqkc', qg, kg).astype(jnp.float32) + bias.astype(jnp.float32)
    s_loc = jnp.where(in_win[:, None, :], s_loc, -jnp.inf).reshape(b, h, rows, GRID_W, kr * GRID_W)
    s_ctx = jnp.einsum('brqhd,bjhd->bhrqj', qg, k_ctx).astype(jnp.float32)
    p = jax.nn.softmax(jnp.concatenate([s_loc, s_ctx], axis=-1), axis=-1).astype(v.dtype)
    n_loc = kr * GRID_W
    o = (jnp.einsum('bhrqj,brjhd->brqhd', p[..., :n_loc], vg)
         + jnp.einsum('bhrqj,bjhd->brqhd', p[..., n_loc:], v_ctx))
    return o.reshape(b, t, h * hd)


def short_conv(u, b_gate, c_gate, conv_w):
    z = c_gate * u
    zp = jnp.pad(z, ((0, 0), (1, 1), (0, 0)))
    y = zp[:, :-2] * conv_w[0] + zp[:, 1:-1] * conv_w[1] + zp[:, 2:] * conv_w[2]
    return b_gate * y


def expert_choice_ffn(h, w_router, w_gate, w_up, w_down):
    b, n, d = h.shape
    cap = CAPACITY_FACTOR * n // N_EXPERTS
    aff = jax.nn.softmax((h @ w_router).astype(jnp.float32), axis=-1)
    g, idx = lax.top_k(aff.transpose(0, 2, 1), cap)
    xs = jax.vmap(lambda hb, ib: hb[ib])(h, idx)
    a = jnp.einsum('becd,edf->becf', xs, w_gate)
    u = jnp.einsum('becd,edf->becf', xs, w_up)
    o = jnp.einsum('becf,efd->becd', jax.nn.silu(a) * u, w_down) * g[..., None].astype(h.dtype)
    return jax.vmap(lambda ib, ob: jnp.zeros((n, d), ob.dtype).at[ib.reshape(-1)].add(ob.reshape(-1, d)))(idx, o)


def trunk_layer(x, mod, attend, norm1_g, norm2_g, w_in, conv_w, w_br_att, w_br_conv, w_out,
                w_router, w_gate, w_up, w_down):
    shift1, scale1, gate1, shift2, scale2, gate2 = mod
    b, n, _ = x.shape
    h = modulated_norm(x, norm1_g, shift1, scale1)
    q, k, v, u, bg, cg, ga, gb = jnp.split(h @ w_in, SPLITS, axis=-1)
    q = q.reshape(b, n, N_HEADS, HEAD_DIM)
    k = k.reshape(b, n, N_HEADS, HEAD_DIM)
    v = v.reshape(b, n, N_HEADS, HEAD_DIM)
    y_att = attend(q, k, v)
    y_conv = short_conv(u, bg, cg, conv_w)
    merged = jax.nn.sigmoid(ga) * (y_att @ w_br_att) + jax.nn.sigmoid(gb) * (y_conv @ w_br_conv)
    x = x + gate1 * (merged @ w_out)
    h2 = modulated_norm(x, norm2_g, shift2, scale2)
    x = x + gate2 * expert_choice_ffn(h2, w_router, w_gate, w_up, w_down)
    return x, k, v


def setup_inputs(seed: int = 0) -> dict:
    key = jax.random.key(seed)
    ks = jax.random.split(key, 24)
    nrm = lambda k, shape, s: jax.random.normal(k, shape, jnp.float32) * s
    return {
        "x_prompt": nrm(ks[0], (BATCH, SEQ, D_MODEL), 1.0),
        "x_sample": nrm(ks[1], (DEC_BATCH, DEC_SEQ, D_MODEL), 1.0),
        "cache_k": nrm(ks[2], (DEC_BATCH, DEPTH, PAST_LEN, N_HEADS, HEAD_DIM), 1.0),
        "cache_v": nrm(ks[3], (DEC_BATCH, DEPTH, PAST_LEN, N_HEADS, HEAD_DIM), 1.0),
        "c": nrm(ks[4], (DEC_BATCH, D_MODEL), 1.0),
        "c_ctx": nrm(ks[5], (D_MODEL,), 1.0),
        "w_ada": nrm(ks[6], (DEPTH, D_MODEL, N_MOD * D_MODEL), 0.5 * D_MODEL ** -0.5),
        "b_ada": nrm(ks[7], (DEPTH, N_MOD * D_MODEL), 0.01),
        "norm1_g": 1.0 + nrm(ks[8], (DEPTH, D_MODEL), 0.1),
        "norm2_g": 1.0 + nrm(ks[9], (DEPTH, D_MODEL), 0.1),
        "w_in": nrm(ks[10], (DEPTH, D_MODEL, IN_DIM), D_MODEL ** -0.5),
        "conv_w": nrm(ks[11], (DEPTH, CONV_WIDTH, CONV_DIM), CONV_WIDTH ** -0.5),
        "rpb": nrm(ks[12], (DEPTH, N_HEADS, 2 * WIN_ROWS - 1, 2 * WIN_COLS - 1), 0.1),
        "w_br_att": nrm(ks[13], (DEPTH, ATT_DIM, D_MODEL), ATT_DIM ** -0.5),
        "w_br_conv": nrm(ks[14], (DEPTH, CONV_DIM, D_MODEL), CONV_DIM ** -0.5),
        "w_out": nrm(ks[15], (DEPTH, D_MODEL, D_MODEL), D_MODEL ** -0.5),
        "w_router": nrm(ks[16], (DEPTH, D_MODEL, N_EXPERTS), D_MODEL ** -0.5),
        "w_gate": nrm(ks[17], (DEPTH, N_EXPERTS, D_MODEL, EXPERT_FF), D_MODEL ** -0.5),
        "w_up": nrm(ks[18], (DEPTH, N_EXPERTS, D_MODEL, EXPERT_FF), D_MODEL ** -0.5),
        "w_down": nrm(ks[19], (DEPTH, N_EXPERTS, EXPERT_FF, D_MODEL), EXPERT_FF ** -0.5),
        "final_g": 1.0 + nrm(ks[20], (D_MODEL,), 0.1),
    }


def reference(x_prompt, x_sample, cache_k, cache_v, c, c_ctx, w_ada, b_ada, norm1_g, norm2_g,
              w_in, conv_w, rpb, w_br_att, w_br_conv, w_out, w_router, w_gate, w_up, w_down, final_g):
    xp, xs = x_prompt, x_sample
    new_k, new_v = [], []
    for l in range(DEPTH):
        weights = (norm1_g[l], norm2_g[l], w_in[l], conv_w[l], w_br_att[l], w_br_conv[l], w_out[l],
                   w_router[l], w_gate[l], w_up[l], w_down[l])
        mod_ctx = adaln(c_ctx[None, :], w_ada[l], b_ada[l])
        mod_lat = adaln(c, w_ada[l], b_ada[l])
        xp, kp, vp = trunk_layer(xp, mod_ctx, context_attention, *weights)
        new_k.append(kp)
        new_v.append(vp)
        ck, cv, bias_l = cache_k[:, l], cache_v[:, l], rpb[l]
        xs, _, _ = trunk_layer(
            xs, mod_lat,
            lambda q, k, v: neighborhood_attention(q, k, v, ck, cv, bias_l),
            *weights)
    y_prompt = rmsnorm(xp, final_g)
    y_sample = rmsnorm(xs, final_g)
    new_cache_k = jnp.stack(new_k, axis=1)
    new_cache_v = jnp.stack(new_v, axis=1)
    return (y_prompt, y_sample, new_cache_k, new_cache_v)
```

```python
import functools

import jax
import jax.numpy as jnp
from jax import lax
from jax.experimental import pallas as pl
from jax.experimental.pallas import tpu as pltpu

BF16 = jnp.bfloat16
F32 = jnp.float32
I32 = jnp.int32

GRID_W = 64
WIN_ROWS = 8
WIN_COLS = 16
CAPACITY_FACTOR = 2
N_MOD = 6
RMS_EPS = 1e-6

LANES = 128
SUBLANES = 8
VMEM_LIMIT_BYTES = 56 * 1024 * 1024

ROW_TILE = 1024
COL_TILE = 512
CONV_COL_TILE = 256
NORM_ROWS = 256
ROW_CHUNK = 16
CHUNK_UNROLL = 4

_NT = (((1,), (1,)), ((), ()))


def _params(*sem):
    return pltpu.CompilerParams(dimension_semantics=sem, vmem_limit_bytes=VMEM_LIMIT_BYTES)


def _mod_row(tile, n_prompt_tiles, tile_rows, dec_seq):
    req = lax.div(jnp.maximum(tile - n_prompt_tiles, 0) * tile_rows, dec_seq)
    return jnp.where(tile < n_prompt_tiles, 0, 1 + req)


def _rmsnorm_f32(x, g):
    return x * lax.rsqrt(jnp.mean(x * x, axis=-1, keepdims=True) + RMS_EPS) * g


def _for_row_chunks(n_rows, fn):
    def body(c, carry):
        fn(pl.ds(pl.multiple_of(c * ROW_CHUNK, ROW_CHUNK), ROW_CHUNK))
        return carry

    lax.fori_loop(0, n_rows // ROW_CHUNK, body, 0, unroll=CHUNK_UNROLL)


def _adaln_kernel(c_ref, w_ref, b_ref, o_ref):
    s = jax.nn.silu(c_ref[...])
    o_ref[...] = jnp.dot(s.astype(BF16), w_ref[...].astype(BF16), preferred_element_type=F32) + b_ref[...]


def _adaln(cond, w_ada, b_ada, layer):
    rows, d = cond.shape
    n = w_ada.shape[-1]
    tn = COL_TILE
    return pl.pallas_call(
        _adaln_kernel,
        grid=(n // tn,),
        in_specs=[pl.BlockSpec((rows, d), lambda j: (0, 0)),
                  pl.BlockSpec((None, d, tn), lambda j: (layer, 0, j)),
                  pl.BlockSpec((None, 1, tn), lambda j: (layer, 0, j))],
        out_specs=pl.BlockSpec((rows, tn), lambda j: (0, j)),
        out_shape=jax.ShapeDtypeStruct((rows, n), F32),
        compiler_params=_params("arbitrary"),
        name="adaln",
    )(cond, w_ada, b_ada.reshape(b_ada.shape[0], 1, n))


def _norm1_kernel(xp_ref, xs_ref, sh_ref, sc_ref, g_ref, o_ref, *, n_p, dec_seq):
    i = pl.program_id(0)
    r = _mod_row(i, n_p, o_ref.shape[0], dec_seq)
    shift = sh_ref[pl.ds(r, 1), :]
    scale = sc_ref[pl.ds(r, 1), :]

    def emit(x_ref):
        def chunk(rs):
            y = _rmsnorm_f32(x_ref[rs, :], g_ref[...])
            o_ref[rs, :] = (y * (1.0 + scale) + shift).astype(o_ref.dtype)

        _for_row_chunks(o_ref.shape[0], chunk)

    pl.when(i < n_p)(lambda: emit(xp_ref))
    pl.when(i >= n_p)(lambda: emit(xs_ref))


def _norm1(xp, xs, mod, g, layer, dec_seq):
    rp, d = xp.shape
    rs = xs.shape[0]
    tr = NORM_ROWS
    n_p, n_s = rp // tr, rs // tr
    mrows = mod.shape[0]
    return pl.pallas_call(
        functools.partial(_norm1_kernel, n_p=n_p, dec_seq=dec_seq),
        grid=(n_p + n_s,),
        in_specs=[pl.BlockSpec((tr, d), lambda i: (jnp.minimum(i, n_p - 1), 0)),
                  pl.BlockSpec((tr, d), lambda i: (jnp.maximum(i - n_p, 0), 0)),
                  pl.BlockSpec((mrows, d), lambda i: (0, 0)),
                  pl.BlockSpec((mrows, d), lambda i: (0, 1)),
                  pl.BlockSpec((None, 1, d), lambda i: (layer, 0, 0))],
        out_specs=pl.BlockSpec((tr, d), lambda i: (i, 0)),
        out_shape=jax.ShapeDtypeStruct((rp + rs, d), BF16),
        compiler_params=_params("arbitrary"),
        name="norm1",
    )(xp, xs, mod, mod, g.reshape(g.shape[0], 1, d))


def _proj_kernel(a_ref, b_ref, o_ref, bs_ref):
    @pl.when(pl.program_id(1) == 0)
    def _():
        bs_ref[...] = b_ref[...].astype(BF16)

    o_ref[...] = jnp.dot(a_ref[...], bs_ref[...], preferred_element_type=F32).astype(o_ref.dtype)


def _proj(a, w, layer, col0, ncols, out_dtype, name):
    m, k = a.shape
    tm, tn = ROW_TILE, COL_TILE
    cb0 = col0 // tn
    return pl.pallas_call(
        _proj_kernel,
        grid=(ncols // tn, m // tm),
        in_specs=[pl.BlockSpec((tm, k), lambda j, i: (i, 0)),
                  pl.BlockSpec((None, k, tn), lambda j, i: (layer, 0, cb0 + j))],
        out_specs=pl.BlockSpec((tm, tn), lambda j, i: (i, j)),
        out_shape=jax.ShapeDtypeStruct((m, ncols), out_dtype),
        scratch_shapes=[pltpu.VMEM((k, tn), BF16)],
        compiler_params=_params("arbitrary", "arbitrary"),
        name=name,
    )(a, w)


def _proj_split_kernel(a_ref, b_ref, op_ref, os_ref, bs_ref, *, n_p):
    i = pl.program_id(1)

    @pl.when(i == 0)
    def _():
        bs_ref[...] = b_ref[...].astype(BF16)

    res = jnp.dot(a_ref[...], bs_ref[...], preferred_element_type=F32)

    @pl.when(i < n_p)
    def _():
        op_ref[...] = res

    @pl.when(i >= n_p)
    def _():
        os_ref[...] = res


def _proj_split(a, w, layer, col0, ncols, n_prompt_rows, name):
    m, k = a.shape
    tm, tn = ROW_TILE, COL_TILE
    cb0 = col0 // tn
    n_p = n_prompt_rows // tm
    return pl.pallas_call(
        functools.partial(_proj_split_kernel, n_p=n_p),
        grid=(ncols // tn, m // tm),
        in_specs=[pl.BlockSpec((tm, k), lambda j, i: (i, 0)),
                  pl.BlockSpec((None, k, tn), lambda j, i: (layer, 0, cb0 + j))],
        out_specs=[pl.BlockSpec((tm, tn), lambda j, i: (jnp.minimum(i, n_p - 1), j)),
                   pl.BlockSpec((tm, tn), lambda j, i: (jnp.maximum(i - n_p, 0), j))],
        out_shape=[jax.ShapeDtypeStruct((n_prompt_rows, ncols), F32),
                   jax.ShapeDtypeStruct((m - n_prompt_rows, ncols), F32)],
        scratch_shapes=[pltpu.VMEM((k, tn), BF16)],
        compiler_params=_params("arbitrary", "arbitrary"),
        name=name,
    )(a, w)


def _seq_pos(rows, seq):
    t = lax.broadcasted_iota(I32, (rows, 1), 0)
    return t & (seq - 1) if seq & (seq - 1) == 0 else lax.rem(t, seq)


def _conv_proj_kernel(a_ref, wu_ref, wb_ref, wc_ref, cw_ref, o_ref, us_ref, bs_ref, cs_ref, *, n_p, seq_p, seq_s):
    i = pl.program_id(1)
    tm = o_ref.shape[0]

    @pl.when(i == 0)
    def _():
        us_ref[...] = wu_ref[...].astype(BF16)
        bs_ref[...] = wb_ref[...].astype(BF16)
        cs_ref[...] = wc_ref[...].astype(BF16)

    a = a_ref[...]
    dot = functools.partial(jnp.dot, preferred_element_type=F32)
    z = dot(a, cs_ref[...]) * dot(a, us_ref[...])
    prompt = i < n_p
    pos = jnp.where(prompt, _seq_pos(tm, seq_p), _seq_pos(tm, seq_s))
    last = jnp.where(prompt, seq_p - 1, seq_s - 1)
    z_prev = jnp.where(pos == 0, 0.0, pltpu.roll(z, 1, 0))
    z_next = jnp.where(pos == last, 0.0, pltpu.roll(z, tm - 1, 0))
    w = cw_ref[...]
    y = z_prev * w[0:1, :] + z * w[1:2, :] + z_next * w[2:3, :]
    o_ref[...] = (dot(a, bs_ref[...]) * y).astype(o_ref.dtype)


def _conv_proj(h, w_in, conv_w, layer, col0, c_dim, n_p_rows, seq_p, seq_s):
    r, k = h.shape
    tm, tn = ROW_TILE, CONV_COL_TILE
    assert tm % seq_p == 0 and tm % seq_s == 0, "row tile must hold whole sequences"
    assert conv_w.shape[1] == 3, "centred width-3 convolution"
    cb0, ncb = col0 // tn, c_dim // tn

    def wspec(part):
        return pl.BlockSpec((None, k, tn), lambda j, i: (layer, 0, cb0 + part * ncb + j))

    return pl.pallas_call(
        functools.partial(_conv_proj_kernel, n_p=n_p_rows // tm, seq_p=seq_p, seq_s=seq_s),
        grid=(ncb, r // tm),
        in_specs=[pl.BlockSpec((tm, k), lambda j, i: (i, 0)), wspec(0), wspec(1), wspec(2),
                  pl.BlockSpec((None, 3, tn), lambda j, i: (layer, 0, j))],
        out_specs=pl.BlockSpec((tm, tn), lambda j, i: (i, j)),
        out_shape=jax.ShapeDtypeStruct((r, c_dim), BF16),
        scratch_shapes=[pltpu.VMEM((k, tn), BF16)] * 3,
        compiler_params=_params("arbitrary", "arbitrary"),
        name="proj_conv",
    )(h, w_in, w_in, w_in, conv_w)


def _softmax_rows(parts):
    m = functools.reduce(jnp.maximum, [jnp.max(s, axis=-1, keepdims=True) for s in parts])
    es = [jnp.exp(s - m) for s in parts]
    inv = 1.0 / functools.reduce(jnp.add, [jnp.sum(e, axis=-1, keepdims=True) for e in es])
    return [e * inv for e in es]


def _ctx_attn_kernel(q_ref, k_ref, v_ref, o_ref, *, n_heads, hd):
    scale = hd ** -0.5
    for h in range(n_heads):
        sl = slice(h * hd, (h + 1) * hd)
        s = lax.dot_general(q_ref[:, sl], k_ref[:, sl].astype(BF16), _NT, preferred_element_type=F32) * scale
        (p,) = _softmax_rows([s])
        o = jnp.dot(p.astype(BF16), v_ref[:, sl].astype(BF16), preferred_element_type=F32)
        o_ref[:, sl] = o.astype(o_ref.dtype)


def _ctx_attention(q, k_p, v_p, n_req, seq, n_heads, hd):
    r, a = q.shape
    return pl.pallas_call(
        functools.partial(_ctx_attn_kernel, n_heads=n_heads, hd=hd),
        grid=(n_req,),
        in_specs=[pl.BlockSpec((seq, a), lambda b: (b, 0))] * 3,
        out_specs=pl.BlockSpec((seq, a), lambda b: (b, 0)),
        out_shape=jax.ShapeDtypeStruct((r, a), BF16),
        compiler_params=_params("arbitrary"),
        name="ctx_attention",
    )(q, k_p, v_p)


def _na_kernel(rpb_ref, q_ref, k_ref, v_ref, ck_ref, cv_ref, _, o_ref, *, layer, n_heads, hd, rows, kr):
    w, wr, wc = GRID_W, WIN_ROWS, WIN_COLS
    ndr, ndc = 2 * wr - 1, 2 * wc - 1
    scale = hd ** -0.5
    base = (layer * n_heads + pl.program_id(1)) * (ndr * ndc)

    qc = lax.broadcasted_iota(I32, (w, 2 * w), 0)
    lane = lax.broadcasted_iota(I32, (w, 2 * w), 1)
    second = lane >= w
    kc = jnp.where(second, lane - w, lane)
    c0 = jnp.clip(qc - wc // 2, 0, w - wc)
    in_win = (kc >= c0) & (kc < c0 + wc)
    dc_idx = jnp.clip(kc - qc + wc - 1, 0, ndc - 1)
    dc_masks = [dc_idx == dc for dc in range(ndc)]
    pairs = []
    for dr in range(ndr - 1):
        t = jnp.zeros((w, 2 * w), F32)
        for dc in range(ndc):
            va = rpb_ref[base + dr * ndc + dc]
            vb = rpb_ref[base + (dr + 1) * ndc + dc]
            t = jnp.where(dc_masks[dc], jnp.where(second, vb, va), t)
        pairs.append(jnp.where(in_win, t, -jnp.inf))

    q = q_ref[...]
    kb = k_ref[...].astype(BF16)
    vb16 = v_ref[...].astype(BF16)
    cvb = cv_ref[...].astype(BF16)
    s_ctx = lax.dot_general(q, ck_ref[...].astype(BF16), _NT, preferred_element_type=F32) * scale

    groups = []
    for r in range(rows):
        r0 = min(max(r - kr // 2, 0), rows - kr)
        if groups and groups[-1][2] == r0:
            groups[-1][1] += 1
        else:
            groups.append([r, 1, r0])

    s_loc = []
    for r_first, cnt, r0 in groups:
        qs = slice(r_first * w, (r_first + cnt) * w)
        ks = slice(r0 * w, (r0 + kr) * w)
        bias = jnp.concatenate(
            [jnp.concatenate([pairs[r0 - r + wr - 1 + 2 * m] for m in range(kr // 2)], axis=1)
             for r in range(r_first, r_first + cnt)], axis=0)
        s_loc.append(lax.dot_general(q[qs], kb[ks], _NT, preferred_element_type=F32) * scale + bias)
    p_loc, p_ctx = _softmax_rows([jnp.concatenate(s_loc, axis=0), s_ctx])
    p_loc = p_loc.astype(BF16)
    o_ctx = jnp.dot(p_ctx.astype(BF16), cvb, preferred_element_type=F32)
    for r_first, cnt, r0 in groups:
        qs = slice(r_first * w, (r_first + cnt) * w)
        ks = slice(r0 * w, (r0 + kr) * w)
        o = jnp.dot(p_loc[qs], vb16[ks], preferred_element_type=F32) + o_ctx[qs]
        o_ref[qs, :] = o.astype(o_ref.dtype)


def _nbr_attention(rpb, q, k_s, v_s, cache_k, cache_v, y_att, layer, n_heads, hd):
    n_dec, _, past, _ = cache_k.shape
    t = k_s.shape[0] // n_dec
    rows = t // GRID_W
    kr = min(WIN_ROWS, rows)
    assert kr % 2 == 0 and t % GRID_W == 0
    blk0 = (q.shape[0] - k_s.shape[0]) // t
    tok = pl.BlockSpec((t, hd), lambda b, h: (b, h))
    tok_q = pl.BlockSpec((t, hd), lambda b, h: (blk0 + b, h))
    cache = pl.BlockSpec((None, None, past, hd), lambda b, h: (b, layer, 0, h))
    return pl.pallas_call(
        functools.partial(_na_kernel, layer=layer, n_heads=n_heads, hd=hd, rows=rows, kr=kr),
        grid=(n_dec, n_heads),
        in_specs=[pl.BlockSpec(memory_space=pltpu.SMEM), tok_q, tok, tok, cache, cache,
                  pl.BlockSpec(memory_space=pl.ANY)],
        out_specs=tok_q,
        out_shape=jax.ShapeDtypeStruct(y_att.shape, y_att.dtype),
        input_output_aliases={6: 0},
        compiler_params=_params("arbitrary", "arbitrary"),
        name="nbr_attention",
    )(rpb.reshape(-1), q, k_s, v_s, cache_k, cache_v, y_att)


def _merge_kernel(ya_ref, yc_ref, ga_ref, gb_ref, wa_ref, wc_ref, o_ref, was_ref, wcs_ref):
    @pl.when(pl.program_id(1) == 0)
    def _():
        was_ref[...] = wa_ref[...].astype(BF16)
        wcs_ref[...] = wc_ref[...].astype(BF16)

    a = jnp.dot(ya_ref[...], was_ref[...], preferred_element_type=F32)
    c = jnp.dot(yc_ref[...], wcs_ref[...], preferred_element_type=F32)
    o = jax.nn.sigmoid(ga_ref[...].astype(F32)) * a + jax.nn.sigmoid(gb_ref[...].astype(F32)) * c
    o_ref[...] = o.astype(o_ref.dtype)


def _merge(y_att, y_conv, rest, w_br_att, w_br_conv, layer, gate_col0, d):
    r, a_dim = y_att.shape
    c_dim = y_conv.shape[1]
    tm, tn = ROW_TILE, COL_TILE
    gcb = gate_col0 // tn
    ndb = d // tn
    return pl.pallas_call(
        _merge_kernel,
        grid=(ndb, r // tm),
        in_specs=[pl.BlockSpec((tm, a_dim), lambda j, i: (i, 0)),
                  pl.BlockSpec((tm, c_dim), lambda j, i: (i, 0)),
                  pl.BlockSpec((tm, tn), lambda j, i: (i, gcb + j)),
                  pl.BlockSpec((tm, tn), lambda j, i: (i, gcb + ndb + j)),
                  pl.BlockSpec((None, a_dim, tn), lambda j, i: (layer, 0, j)),
                  pl.BlockSpec((None, c_dim, tn), lambda j, i: (layer, 0, j))],
        out_specs=pl.BlockSpec((tm, tn), lambda j, i: (i, j)),
        out_shape=jax.ShapeDtypeStruct((r, d), BF16),
        scratch_shapes=[pltpu.VMEM((a_dim, tn), BF16), pltpu.VMEM((c_dim, tn), BF16)],
        compiler_params=_params("arbitrary", "arbitrary"),
        name="merge",
    )(y_att, y_conv, rest, rest, w_br_att, w_br_conv)


def _outproj_kernel(m_ref, w_ref, xp_ref, xs_ref, gate_ref, o_ref, ws_ref, *, n_p, dec_seq):
    i = pl.program_id(1)

    @pl.when(i == 0)
    def _():
        ws_ref[...] = w_ref[...].astype(BF16)

    gate = gate_ref[pl.ds(_mod_row(i, n_p, o_ref.shape[0], dec_seq), 1), :]
    upd = gate * jnp.dot(m_ref[...], ws_ref[...], preferred_element_type=F32)

    @pl.when(i < n_p)
    def _():
        o_ref[...] = xp_ref[...] + upd

    @pl.when(i >= n_p)
    def _():
        o_ref[...] = xs_ref[...] + upd


def _outproj(merged, w_out, xp, xs, mod, layer, dec_seq):
    r, d = merged.shape
    tm, tn = ROW_TILE, COL_TILE
    n_p = xp.shape[0] // tm
    mrows = mod.shape[0]
    ndb = d // tn
    return pl.pallas_call(
        functools.partial(_outproj_kernel, n_p=n_p, dec_seq=dec_seq),
        grid=(ndb, r // tm),
        in_specs=[pl.BlockSpec((tm, d), lambda j, i: (i, 0)),
                  pl.BlockSpec((None, d, tn), lambda j, i: (layer, 0, j)),
                  pl.BlockSpec((tm, tn), lambda j, i: (jnp.minimum(i, n_p - 1), j)),
                  pl.BlockSpec((tm, tn), lambda j, i: (jnp.maximum(i - n_p, 0), j)),
                  pl.BlockSpec((mrows, tn), lambda j, i: (0, 2 * ndb + j))],
        out_specs=pl.BlockSpec((tm, tn), lambda j, i: (i, j)),
        out_shape=jax.ShapeDtypeStruct((r, d), F32),
        scratch_shapes=[pltpu.VMEM((d, tn), BF16)],
        compiler_params=_params("arbitrary", "arbitrary"),
        name="outproj",
    )(merged, w_out, xp, xs, mod)


def _split_bf16(x):
    hi = x.astype(BF16)
    return hi, (x - hi.astype(F32)).astype(BF16)


def _norm2_kernel(x_ref, sh_ref, sc_ref, g_ref, wr_ref, h_ref, lg_ref, w2_ref, *, n_p, dec_seq):
    i = pl.program_id(0)
    ep = lg_ref.shape[1]

    @pl.when(i == 0)
    def _():
        w_hi, w_lo = _split_bf16(wr_ref[...])
        w2_ref[:, :ep] = w_hi
        w2_ref[:, ep:] = w_lo

    r = _mod_row(i, n_p, h_ref.shape[0], dec_seq)
    y = _rmsnorm_f32(x_ref[...], g_ref[...])
    h = y * (1.0 + sc_ref[pl.ds(r, 1), :]) + sh_ref[pl.ds(r, 1), :]
    h_hi, h_lo = _split_bf16(h)
    h_ref[...] = h_hi
    p_hi = jnp.dot(h_hi, w2_ref[...], preferred_element_type=F32)
    p_lo = jnp.dot(h_lo, w2_ref[...], preferred_element_type=F32)
    lg_ref[...] = p_hi[:, :ep] + (p_hi[:, ep:] + (p_lo[:, :ep] + p_lo[:, ep:]))


def _norm2(x1, mod, g, w_router_pad, layer, n_p_rows, dec_seq):
    r, d = x1.shape
    tr = NORM_ROWS
    mrows = mod.shape[0]
    ep = w_router_pad.shape[-1]
    return pl.pallas_call(
        functools.partial(_norm2_kernel, n_p=n_p_rows // tr, dec_seq=dec_seq),
        grid=(r // tr,),
        in_specs=[pl.BlockSpec((tr, d), lambda i: (i, 0)),
                  pl.BlockSpec((mrows, d), lambda i: (0, 3)),
                  pl.BlockSpec((mrows, d), lambda i: (0, 4)),
                  pl.BlockSpec((None, 1, d), lambda i: (layer, 0, 0)),
                  pl.BlockSpec((None, d, ep), lambda i: (layer, 0, 0))],
        out_specs=[pl.BlockSpec((tr, d), lambda i: (i, 0)),
                   pl.BlockSpec((tr, ep), lambda i: (i, 0))],
        out_shape=[jax.ShapeDtypeStruct((r, d), BF16), jax.ShapeDtypeStruct((r, ep), F32)],
        scratch_shapes=[pltpu.VMEM((d, 2 * ep), BF16)],
        compiler_params=_params("arbitrary"),
        name="norm2_router",
    )(x1, mod, mod, g.reshape(g.shape[0], 1, d), w_router_pad)


def _route_kernel(lg_ref, h_ref, *refs, n, cap, n_exp, aliased):
    xs_ref, st_ref, g_ref, s_scr, s32_scr, afft_scr = refs[2 if aliased else 0:]
    ep = lg_ref.shape[1]

    @pl.when(pl.program_id(1) == 0)
    def _():
        lane = lax.broadcasted_iota(I32, (n, ep), 1)
        lg = jnp.where(lane < n_exp, lg_ref[...], -jnp.inf)
        ex = jnp.exp(lg - jnp.max(lg, axis=-1, keepdims=True))
        aff = ex / jnp.sum(ex, axis=-1, keepdims=True)
        afft_scr[...] = aff.T
        slot_id = lax.broadcasted_iota(I32, (cap, n), 0).astype(F32)

        def per_expert(e, carry):
            row = afft_scr[pl.ds(e, 1), :]
            col = jnp.sum(jnp.where(lane == e, aff, 0.0), axis=1, keepdims=True)
            earlier = lax.broadcasted_iota(I32, (n, n), 0) < lax.broadcasted_iota(I32, (n, n), 1)
            beats = jnp.where(earlier, jnp.where(col >= row, 1.0, 0.0), jnp.where(col > row, 1.0, 0.0))
            rank = jnp.sum(beats, axis=0, keepdims=True)
            s_e = (rank == slot_id).astype(F32)
            s32_scr[pl.ds(pl.multiple_of(e * cap, cap), cap), :] = s_e
            g_ref[e] = jnp.sum(s_e * row, axis=1, keepdims=True)
            return carry

        lax.fori_loop(0, n_exp, per_expert, 0, unroll=max(1, 1024 // n))
        s32 = s32_scr[...]
        s_scr[...] = s32.astype(BF16)
        st_ref[...] = s32.T.astype(st_ref.dtype)

    x = jnp.dot(s_scr[...], h_ref[...], preferred_element_type=F32)
    xs_ref[...] = x.astype(xs_ref.dtype).reshape(xs_ref.shape)


def _route(logits, h2, row0, n_req, n, n_exp, total_slots, prev):
    d = h2.shape[1]
    cap = CAPACITY_FACTOR * n // n_exp
    ecap = n_exp * cap
    ep = logits.shape[1]
    td = d if n * d <= 1024 * 1024 else 1024
    rb0 = row0 // n
    sb0 = 0 if prev is None else prev[2]
    ins = [logits, h2]
    in_specs = [pl.BlockSpec((n, ep), lambda b, t: (rb0 + b, 0)),
                pl.BlockSpec((n, td), lambda b, t: (rb0 + b, t))]
    aliases = {}
    if prev is not None:
        ins += [prev[0], prev[1]]
        in_specs += [pl.BlockSpec(memory_space=pl.ANY)] * 2
        aliases = {2: 0, 3: 2}
    xs, st, g = pl.pallas_call(
        functools.partial(_route_kernel, n=n, cap=cap, n_exp=n_exp, aliased=prev is not None),
        grid=(n_req, d // td),
        in_specs=in_specs,
        out_specs=[pl.BlockSpec((n_exp, cap, td), lambda b, t: (0, sb0 + b, t)),
                   pl.BlockSpec((n, ecap), lambda b, t: (b, 0)),
                   pl.BlockSpec((n_exp, cap, 1), lambda b, t: (0, sb0 + b, 0))],
        out_shape=[jax.ShapeDtypeStruct((n_exp, total_slots, d), BF16),
                   jax.ShapeDtypeStruct((n_req * n, ecap), BF16),
                   jax.ShapeDtypeStruct((n_exp, total_slots, 1), F32)],
        scratch_shapes=[pltpu.VMEM((ecap, n), BF16), pltpu.VMEM((ecap, n), F32), pltpu.VMEM((ep, n), F32)],
        input_output_aliases=aliases,
        compiler_params=_params("arbitrary", "arbitrary"),
        name="route_gather_n%d" % n,
    )(*ins)
    return xs, st, g


def _ffn_up_kernel(x_ref, wg_ref, wu_ref, o_ref):
    x = x_ref[...]
    a = jnp.dot(x, wg_ref[...].astype(BF16), preferred_element_type=F32)
    u = jnp.dot(x, wu_ref[...].astype(BF16), preferred_element_type=F32)
    o_ref[...] = (jax.nn.silu(a) * u).astype(o_ref.dtype)


def _ffn_up(xs, w_gate, w_up, layer):
    n_exp, slots, d = xs.shape
    f = w_gate.shape[-1]
    tf = 256
    wspec = pl.BlockSpec((None, None, d, tf), lambda e, j: (layer, e, 0, j))
    return pl.pallas_call(
        _ffn_up_kernel,
        grid=(n_exp, f // tf),
        in_specs=[pl.BlockSpec((None, slots, d), lambda e, j: (e, 0, 0)), wspec, wspec],
        out_specs=pl.BlockSpec((None, slots, tf), lambda e, j: (e, 0, j)),
        out_shape=jax.ShapeDtypeStruct((n_exp, slots, f), BF16),
        compiler_params=_params("arbitrary", "arbitrary"),
        name="ffn_up",
    )(xs, w_gate, w_up)


def _ffn_down_kernel(h_ref, w_ref, g_ref, o_ref):
    o = jnp.dot(h_ref[...], w_ref[...].astype(BF16), preferred_element_type=F32)
    o_ref[...] = (o * g_ref[...]).astype(o_ref.dtype)


def _ffn_down(hmid, w_down, g, layer):
    n_exp, slots, f = hmid.shape
    d = w_down.shape[-1]
    tn = COL_TILE
    return pl.pallas_call(
        _ffn_down_kernel,
        grid=(n_exp, d // tn),
        in_specs=[pl.BlockSpec((None, slots, f), lambda e, j: (e, 0, 0)),
                  pl.BlockSpec((None, None, f, tn), lambda e, j: (layer, e, 0, j)),
                  pl.BlockSpec((None, slots, 1), lambda e, j: (e, 0, 0))],
        out_specs=pl.BlockSpec((None, slots, tn), lambda e, j: (e, 0, j)),
        out_shape=jax.ShapeDtypeStruct((n_exp, slots, d), BF16),
        compiler_params=_params("arbitrary", "arbitrary"),
        name="ffn_down",
    )(hmid, w_down, g)


def _combine_kernel(x_ref, st_ref, o_in_ref, gate_ref, fg_ref, o_ref, *, n_k, per_req_mod, final):
    kk = pl.program_id(2)
    mod_row = 1 + pl.program_id(0) if per_req_mod else 0
    part = jnp.dot(st_ref[...], o_in_ref[...].reshape(st_ref.shape[1], o_ref.shape[1]),
                   preferred_element_type=F32)

    @pl.when(kk == 0)
    def _():
        o_ref[...] = part

    @pl.when(kk > 0)
    def _():
        o_ref[...] += part

    @pl.when(kk == n_k - 1)
    def _():
        x2 = x_ref[...] + gate_ref[pl.ds(mod_row, 1), :] * o_ref[...]
        o_ref[...] = _rmsnorm_f32(x2, fg_ref[...]) if final else x2


def _combine(x1, st, o_exp, mod, final_g, row0, n_req, n, slot_blk0, per_req_mod, final):
    d = x1.shape[1]
    n_exp = o_exp.shape[0]
    ecap = st.shape[1]
    cap = ecap // n_exp
    tr = min(n, NORM_ROWS)
    e_chunk = max(1, min(n_exp, 512 // cap))
    n_k = n_exp // e_chunk
    rb0 = row0 // tr
    rt = n // tr
    mrows = mod.shape[0]
    return pl.pallas_call(
        functools.partial(_combine_kernel, n_k=n_k, per_req_mod=per_req_mod, final=final),
        grid=(n_req, rt, n_k),
        in_specs=[pl.BlockSpec((tr, d), lambda b, t, k: (rb0 + b * rt + t, 0)),
                  pl.BlockSpec((tr, e_chunk * cap), lambda b, t, k: (b * rt + t, k)),
                  pl.BlockSpec((e_chunk, cap, d), lambda b, t, k: (k, slot_blk0 + b, 0)),
                  pl.BlockSpec((mrows, d), lambda b, t, k: (0, 5)),
                  pl.BlockSpec((1, d), lambda b, t, k: (0, 0))],
        out_specs=pl.BlockSpec((tr, d), lambda b, t, k: (b * rt + t, 0)),
        out_shape=jax.ShapeDtypeStruct((n_req * n, d), F32),
        compiler_params=_params("arbitrary", "arbitrary", "arbitrary"),
        name="combine_n%d" % n,
    )(x1, st, o_exp, mod, final_g.reshape(1, d))


def kernel(x_prompt, x_sample, cache_k, cache_v, c, c_ctx, w_ada, b_ada, norm1_g, norm2_g, w_in, conv_w, rpb, w_br_att, w_br_conv, w_out, w_router, w_gate, w_up, w_down, final_g):
    n_req, seq, d = x_prompt.shape
    n_dec, dec_seq, _ = x_sample.shape
    depth = w_in.shape[0]
    n_heads, hd = cache_k.shape[-2:]
    a_dim = n_heads * hd
    c_dim = conv_w.shape[-1]
    n_exp = w_router.shape[-1]
    rp, rs = n_req * seq, n_dec * dec_seq
    tm = ROW_TILE
    assert rp % tm == 0 and rs % tm == 0 and dec_seq % NORM_ROWS == 0 and seq % NORM_ROWS == 0

    xp = x_prompt.reshape(rp, d)
    xs = x_sample.reshape(rs, d)
    ck = cache_k.reshape(cache_k.shape[:3] + (a_dim,))
    cv = cache_v.reshape(cache_v.shape[:3] + (a_dim,))
    cond = jnp.concatenate([c_ctx[None, :], c], axis=0)
    cond = jnp.pad(cond, ((0, -cond.shape[0] % SUBLANES), (0, 0)))
    w_router_pad = jnp.pad(w_router, ((0, 0), (0, 0), (0, -n_exp % LANES)))
    cap_p = CAPACITY_FACTOR * seq // n_exp
    cap_s = CAPACITY_FACTOR * dec_seq // n_exp
    slots = n_req * cap_p + n_dec * cap_s

    new_k, new_v = [], []
    for l in range(depth):
        last = l == depth - 1
        mod = _adaln(cond, w_ada, b_ada, l)
        h = _norm1(xp, xs, mod, norm1_g, l, dec_seq)
        q = _proj(h, w_in, l, 0, a_dim, BF16, "proj_q")
        k_p, k_s = _proj_split(h, w_in, l, a_dim, a_dim, rp, "proj_k")
        v_p, v_s = _proj_split(h, w_in, l, 2 * a_dim, a_dim, rp, "proj_v")
        y_conv = _conv_proj(h, w_in, conv_w, l, 3 * a_dim, c_dim, rp, seq, dec_seq)
        gates = _proj(h, w_in, l, 3 * a_dim + 3 * c_dim, 2 * d, BF16, "proj_gates")
        new_k.append(k_p.reshape(n_req, seq, n_heads, hd))
        new_v.append(v_p.reshape(n_req, seq, n_heads, hd))

        y_att = _ctx_attention(q, k_p, v_p, n_req, seq, n_heads, hd)
        y_att = _nbr_attention(rpb, q, k_s, v_s, ck, cv, y_att, l, n_heads, hd)
        merged = _merge(y_att, y_conv, gates, w_br_att, w_br_conv, l, 0, d)
        x1 = _outproj(merged, w_out, xp, xs, mod, l, dec_seq)

        h2, logits = _norm2(x1, mod, norm2_g, w_router_pad, l, rp, dec_seq)
        xg, st_p, g = _route(logits, h2, 0, n_req, seq, n_exp, slots, None)
        xg, st_s, g = _route(logits, h2, rp, n_dec, dec_seq, n_exp, slots,
                             (xg, g, n_req * cap_p // cap_s))
        o_exp = _ffn_down(_ffn_up(xg, w_gate, w_up, l), w_down, g, l)
        xp = _combine(x1, st_p, o_exp, mod, final_g, 0, n_req, seq, 0, False, last)
        xs = _combine(x1, st_s, o_exp, mod, final_g, rp, n_dec, dec_seq,
                      n_req * cap_p // cap_s, True, last)

    y_prompt = xp.reshape(n_req, seq, d)
    y_sample = xs.reshape(n_dec, dec_seq, d)
    return (y_prompt, y_sample, jnp.stack(new_k, axis=1), jnp.stack(new_v, axis=1))
```

```python
import functools

import jax
import jax.numpy as jnp
from jax import lax
from jax.experimental import pallas as pl
from jax.experimental.pallas import tpu as pltpu

BF16 = jnp.bfloat16
F32 = jnp.float32
I32 = jnp.int32

GRID_W = 64
WIN_ROWS = 8
WIN_COLS = 16
CAPACITY_FACTOR = 2
N_MOD = 6
RMS_EPS = 1e-6

LANES = 128
SUBLANES = 8
VMEM_LIMIT_BYTES = 56 * 1024 * 1024

ROW_TILE = 1024
COL_TILE = 512
CONV_COL_TILE = 256
NORM_ROWS = 256
ROW_CHUNK = 16
CHUNK_UNROLL = 4

_NT = (((1,), (1,)), ((), ()))


def _params(*sem):
    return pltpu.CompilerParams(dimension_semantics=sem, vmem_limit_bytes=VMEM_LIMIT_BYTES)


def _mod_row(tile, n_prompt_tiles, tile_rows, dec_seq):
    req = lax.div(jnp.maximum(tile - n_prompt_tiles, 0) * tile_rows, dec_seq)
    return jnp.where(tile < n_prompt_tiles, 0, 1 + req)


def _rmsnorm_f32(x, g):
    return x * lax.rsqrt(jnp.mean(x * x, axis=-1, keepdims=True) + RMS_EPS) * g


def _for_row_chunks(n_rows, fn):
    def body(c, carry):
        fn(pl.ds(pl.multiple_of(c * ROW_CHUNK, ROW_CHUNK), ROW_CHUNK))
        return carry

    lax.fori_loop(0, n_rows // ROW_CHUNK, body, 0, unroll=CHUNK_UNROLL)


def _adaln_kernel(c_ref, w_ref, b_ref, o_ref):
    s = jax.nn.silu(c_ref[...])
    o_ref[...] = jnp.dot(s.astype(BF16), w_ref[...].astype(BF16), preferred_element_type=F32) + b_ref[...]


def _adaln(cond, w_ada, b_ada, layer, n):
    rows, d = cond.shape
    tn = COL_TILE
    return pl.pallas_call(
        _adaln_kernel,
        grid=(n // tn,),
        in_specs=[pl.BlockSpec((rows, d), lambda j: (0, 0)),
                  pl.BlockSpec((None, d, tn), lambda j: (layer, 0, j)),
                  pl.BlockSpec((None, 1, tn), lambda j: (layer, 0, j))],
        out_specs=pl.BlockSpec((rows, tn), lambda j: (0, j)),
        out_shape=jax.ShapeDtypeStruct((rows, n), F32),
        compiler_params=_params("arbitrary"),
        name="adaln",
    )(cond, w_ada, b_ada.reshape(b_ada.shape[0], 1, -1))


def _norm1_kernel(xp_ref, xs_ref, sh_ref, sc_ref, g_ref, o_ref, *, n_p, dec_seq):
    i = pl.program_id(0)
    r = _mod_row(i, n_p, o_ref.shape[0], dec_seq)
    shift = sh_ref[pl.ds(r, 1), :]
    scale = sc_ref[pl.ds(r, 1), :]

    def emit(x_ref):
        def chunk(rs):
            y = _rmsnorm_f32(x_ref[rs, :], g_ref[...])
            o_ref[rs, :] = (y * (1.0 + scale) + shift).astype(o_ref.dtype)

        _for_row_chunks(o_ref.shape[0], chunk)

    pl.when(i < n_p)(lambda: emit(xp_ref))
    pl.when(i >= n_p)(lambda: emit(xs_ref))


def _norm1(xp, xs, mod, g, layer, dec_seq):
    rp, d = xp.shape
    rs = xs.shape[0]
    tr = 2 * NORM_ROWS
    assert dec_seq % tr == 0 and rp % tr == 0
    n_p, n_s = rp // tr, rs // tr
    mrows = mod.shape[0]
    return pl.pallas_call(
        functools.partial(_norm1_kernel, n_p=n_p, dec_seq=dec_seq),
        grid=(n_p + n_s,),
        in_specs=[pl.BlockSpec((tr, d), lambda i: (jnp.minimum(i, n_p - 1), 0)),
                  pl.BlockSpec((tr, d), lambda i: (jnp.maximum(i - n_p, 0), 0)),
                  pl.BlockSpec((mrows, d), lambda i: (0, 0)),
                  pl.BlockSpec((mrows, d), lambda i: (0, 1)),
                  pl.BlockSpec((None, 1, d), lambda i: (layer, 0, 0))],
        out_specs=pl.BlockSpec((tr, d), lambda i: (i, 0)),
        out_shape=jax.ShapeDtypeStruct((rp + rs, d), BF16),
        compiler_params=_params("arbitrary"),
        name="norm1",
    )(xp, xs, mod, mod, g.reshape(g.shape[0], 1, d))


def _proj_kernel(a_ref, b_ref, o_ref, bs_ref):
    @pl.when(pl.program_id(1) == 0)
    def _():
        bs_ref[...] = b_ref[...].astype(BF16)

    o_ref[...] = jnp.dot(a_ref[...], bs_ref[...], preferred_element_type=F32).astype(o_ref.dtype)


def _proj(a, w, layer, col0, ncols, out_dtype, name):
    m, k = a.shape
    tm, tn = ROW_TILE, COL_TILE
    cb0 = col0 // tn
    return pl.pallas_call(
        _proj_kernel,
        grid=(ncols // tn, m // tm),
        in_specs=[pl.BlockSpec((tm, k), lambda j, i: (i, 0)),
                  pl.BlockSpec((None, k, tn), lambda j, i: (layer, 0, cb0 + j))],
        out_specs=pl.BlockSpec((tm, tn), lambda j, i: (i, j)),
        out_shape=jax.ShapeDtypeStruct((m, ncols), out_dtype),
        scratch_shapes=[pltpu.VMEM((k, tn), BF16)],
        compiler_params=_params("arbitrary", "arbitrary"),
        name=name,
    )(a, w)


def _proj_ada_kernel(a_ref, b_ref, ct_ref, wa_ref, ba_ref, o_ref, mod_ref, bs_ref, sb_ref, *, n_cond):
    j, i = pl.program_id(0), pl.program_id(1)

    @pl.when(i == 0)
    def _():
        bs_ref[...] = b_ref[...].astype(BF16)

    @pl.when((i == 0) & (j == 0))
    def _():
        s_t = jax.nn.silu(ct_ref[...])
        for r in range(n_cond):
            sb_ref[r] = jnp.broadcast_to(s_t[:, r:r + 1], sb_ref.shape[1:])

    o_ref[...] = jnp.dot(a_ref[...], bs_ref[...], preferred_element_type=F32).astype(o_ref.dtype)
    w = wa_ref[...]
    sub = lax.broadcasted_iota(I32, mod_ref.shape, 0)
    acc = jnp.zeros(mod_ref.shape, F32)
    for r in range(n_cond):
        row = jnp.sum(sb_ref[r] * w, axis=0, keepdims=True)
        acc = jnp.where(sub == r, row, acc)
    mod_ref[...] = acc + ba_ref[...]


def _proj_ada(a, w, layer, col0, ncols, cond_t, n_cond, w_ada, b_ada, ada_col0, name):
    m, k = a.shape
    d, crow = cond_t.shape
    tm, tn = ROW_TILE, COL_TILE
    cb0 = col0 // tn
    n_i = m // tm
    n_ada = w_ada.shape[-1] - ada_col0
    n_blocks = n_ada // LANES
    assert n_blocks <= (ncols // tn) * n_i and ada_col0 % LANES == 0
    ab0 = ada_col0 // LANES

    def ada_blk(j, i):
        return ab0 + jnp.minimum(j * n_i + i, n_blocks - 1)

    return pl.pallas_call(
        functools.partial(_proj_ada_kernel, n_cond=n_cond),
        grid=(ncols // tn, n_i),
        in_specs=[pl.BlockSpec((tm, k), lambda j, i: (i, 0)),
                  pl.BlockSpec((None, k, tn), lambda j, i: (layer, 0, cb0 + j)),
                  pl.BlockSpec((d, crow), lambda j, i: (0, 0)),
                  pl.BlockSpec((None, d, LANES), lambda j, i: (layer, 0, ada_blk(j, i))),
                  pl.BlockSpec((None, 1, LANES), lambda j, i: (layer, 0, ada_blk(j, i)))],
        out_specs=[pl.BlockSpec((tm, tn), lambda j, i: (i, j)),
                   pl.BlockSpec((crow, LANES), lambda j, i: (0, ada_blk(j, i) - ab0))],
        out_shape=[jax.ShapeDtypeStruct((m, ncols), BF16),
                   jax.ShapeDtypeStruct((crow, n_ada), F32)],
        scratch_shapes=[pltpu.VMEM((k, tn), BF16), pltpu.VMEM((n_cond, d, LANES), F32)],
        compiler_params=_params("arbitrary", "arbitrary"),
        name=name,
    )(a, w, cond_t, w_ada, b_ada.reshape(b_ada.shape[0], 1, -1))


def _proj_split_kernel(a_ref, b_ref, op_ref, os_ref, bs_ref, *, n_p):
    i = pl.program_id(1)

    @pl.when(i == 0)
    def _():
        bs_ref[...] = b_ref[...].astype(BF16)

    res = jnp.dot(a_ref[...], bs_ref[...], preferred_element_type=F32)

    @pl.when(i < n_p)
    def _():
        op_ref[...] = res

    @pl.when(i >= n_p)
    def _():
        os_ref[...] = res


def _proj_split(a, w, layer, col0, ncols, n_prompt_rows, name):
    m, k = a.shape
    tm, tn = ROW_TILE, COL_TILE
    cb0 = col0 // tn
    n_p = n_prompt_rows // tm
    return pl.pallas_call(
        functools.partial(_proj_split_kernel, n_p=n_p),
        grid=(ncols // tn, m // tm),
        in_specs=[pl.BlockSpec((tm, k), lambda j, i: (i, 0)),
                  pl.BlockSpec((None, k, tn), lambda j, i: (layer, 0, cb0 + j))],
        out_specs=[pl.BlockSpec((tm, tn), lambda j, i: (jnp.minimum(i, n_p - 1), j)),
                   pl.BlockSpec((tm, tn), lambda j, i: (jnp.maximum(i - n_p, 0), j))],
        out_shape=[jax.ShapeDtypeStruct((n_prompt_rows, ncols), F32),
                   jax.ShapeDtypeStruct((m - n_prompt_rows, ncols), F32)],
        scratch_shapes=[pltpu.VMEM((k, tn), BF16)],
        compiler_params=_params("arbitrary", "arbitrary"),
        name=name,
    )(a, w)


def _seq_pos(rows, seq):
    t = lax.broadcasted_iota(I32, (rows, 1), 0)
    return t & (seq - 1) if seq & (seq - 1) == 0 else lax.rem(t, seq)


def _conv_proj_kernel(a_ref, wu_ref, wb_ref, wc_ref, cw_ref, o_ref, us_ref, bs_ref, cs_ref, *, n_p, seq_p, seq_s):
    i = pl.program_id(1)
    tm = o_ref.shape[0]

    @pl.when(i == 0)
    def _():
        us_ref[...] = wu_ref[...].astype(BF16)
        bs_ref[...] = wb_ref[...].astype(BF16)
        cs_ref[...] = wc_ref[...].astype(BF16)

    a = a_ref[...]
    dot = functools.partial(jnp.dot, preferred_element_type=F32)
    z = dot(a, cs_ref[...]) * dot(a, us_ref[...])
    prompt = i < n_p
    pos = jnp.where(prompt, _seq_pos(tm, seq_p), _seq_pos(tm, seq_s))
    last = jnp.where(prompt, seq_p - 1, seq_s - 1)
    z_prev = jnp.where(pos == 0, 0.0, pltpu.roll(z, 1, 0))
    z_next = jnp.where(pos == last, 0.0, pltpu.roll(z, tm - 1, 0))
    w = cw_ref[...]
    y = z_prev * w[0:1, :] + z * w[1:2, :] + z_next * w[2:3, :]
    o_ref[...] = (dot(a, bs_ref[...]) * y).astype(o_ref.dtype)


def _conv_proj(h, w_in, conv_w, layer, col0, c_dim, n_p_rows, seq_p, seq_s):
    r, k = h.shape
    tm, tn = ROW_TILE, CONV_COL_TILE
    assert tm % seq_p == 0 and tm % seq_s == 0, "row tile must hold whole sequences"
    assert conv_w.shape[1] == 3, "centred width-3 convolution"
    cb0, ncb = col0 // tn, c_dim // tn

    def wspec(part):
        return pl.BlockSpec((None, k, tn), lambda j, i: (layer, 0, cb0 + part * ncb + j))

    return pl.pallas_call(
        functools.partial(_conv_proj_kernel, n_p=n_p_rows // tm, seq_p=seq_p, seq_s=seq_s),
        grid=(ncb, r // tm),
        in_specs=[pl.BlockSpec((tm, k), lambda j, i: (i, 0)), wspec(0), wspec(1), wspec(2),
                  pl.BlockSpec((None, 3, tn), lambda j, i: (layer, 0, j))],
        out_specs=pl.BlockSpec((tm, tn), lambda j, i: (i, j)),
        out_shape=jax.ShapeDtypeStruct((r, c_dim), BF16),
        scratch_shapes=[pltpu.VMEM((k, tn), BF16)] * 3,
        compiler_params=_params("arbitrary", "arbitrary"),
        name="proj_conv",
    )(h, w_in, w_in, w_in, conv_w)


def _softmax_rows(parts):
    m = functools.reduce(jnp.maximum, [jnp.max(s, axis=-1, keepdims=True) for s in parts])
    es = [jnp.exp(s - m) for s in parts]
    inv = 1.0 / functools.reduce(jnp.add, [jnp.sum(e, axis=-1, keepdims=True) for e in es])
    return [e * inv for e in es]


def _ctx_attn_kernel(q_ref, k_ref, v_ref, o_ref, *, n_heads, hd):
    scale = hd ** -0.5
    for h in range(n_heads):
        sl = slice(h * hd, (h + 1) * hd)
        s = lax.dot_general(q_ref[:, sl], k_ref[:, sl].astype(BF16), _NT, preferred_element_type=F32) * scale
        (p,) = _softmax_rows([s])
        o = jnp.dot(p.astype(BF16), v_ref[:, sl].astype(BF16), preferred_element_type=F32)
        o_ref[:, sl] = o.astype(o_ref.dtype)


def _ctx_attention(q, k_p, v_p, n_req, seq, n_heads, hd):
    r, a = q.shape
    return pl.pallas_call(
        functools.partial(_ctx_attn_kernel, n_heads=n_heads, hd=hd),
        grid=(n_req,),
        in_specs=[pl.BlockSpec((seq, a), lambda b: (b, 0))] * 3,
        out_specs=pl.BlockSpec((seq, a), lambda b: (b, 0)),
        out_shape=jax.ShapeDtypeStruct((r, a), BF16),
        compiler_params=_params("arbitrary"),
        name="ctx_attention",
    )(q, k_p, v_p)


def _na_kernel(rpb_ref, q_ref, k_ref, v_ref, ck_ref, cv_ref, _, o_ref, pairs_ref, *, layer, n_heads, hd, rows, kr):
    w, wr, wc = GRID_W, WIN_ROWS, WIN_COLS
    ndr, ndc = 2 * wr - 1, 2 * wc - 1
    scale = hd ** -0.5

    @pl.when(pl.program_id(1) == 0)
    def _():
        base = (layer * n_heads + pl.program_id(0)) * (ndr * ndc)
        qc = lax.broadcasted_iota(I32, (w, 2 * w), 0)
        lane = lax.broadcasted_iota(I32, (w, 2 * w), 1)
        second = lane >= w
        kc = jnp.where(second, lane - w, lane)
        c0 = jnp.clip(qc - wc // 2, 0, w - wc)
        in_win = (kc >= c0) & (kc < c0 + wc)
        dc_idx = jnp.clip(kc - qc + wc - 1, 0, ndc - 1)
        dc_masks = [dc_idx == dc for dc in range(ndc)]
        for dr in range(ndr - 1):
            t = jnp.zeros((w, 2 * w), F32)
            for dc in range(ndc):
                va = rpb_ref[base + dr * ndc + dc]
                vb = rpb_ref[base + (dr + 1) * ndc + dc]
                t = jnp.where(dc_masks[dc], jnp.where(second, vb, va), t)
            pairs_ref[dr] = jnp.where(in_win, t, -jnp.inf)

    q = q_ref[...]
    kb = k_ref[...].astype(BF16)
    vb16 = v_ref[...].astype(BF16)
    cvb = cv_ref[...].astype(BF16)
    s_ctx = lax.dot_general(q, ck_ref[...].astype(BF16), _NT, preferred_element_type=F32) * scale

    groups = []
    for r in range(rows):
        r0 = min(max(r - kr // 2, 0), rows - kr)
        if groups and groups[-1][2] == r0:
            groups[-1][1] += 1
        else:
            groups.append([r, 1, r0])

    s_loc = []
    for r_first, cnt, r0 in groups:
        qs = slice(r_first * w, (r_first + cnt) * w)
        ks = slice(r0 * w, (r0 + kr) * w)
        bias = jnp.concatenate(
            [jnp.concatenate([pairs_ref[r0 - r + wr - 1 + 2 * m] for m in range(kr // 2)], axis=1)
             for r in range(r_first, r_first + cnt)], axis=0)
        s_loc.append(lax.dot_general(q[qs], kb[ks], _NT, preferred_element_type=F32) * scale + bias)
    p_loc, p_ctx = _softmax_rows([jnp.concatenate(s_loc, axis=0), s_ctx])
    p_loc = p_loc.astype(BF16)
    o_ctx = jnp.dot(p_ctx.astype(BF16), cvb, preferred_element_type=F32)
    for r_first, cnt, r0 in groups:
        qs = slice(r_first * w, (r_first + cnt) * w)
        ks = slice(r0 * w, (r0 + kr) * w)
        o = jnp.dot(p_loc[qs], vb16[ks], preferred_element_type=F32) + o_ctx[qs]
        o_ref[qs, :] = o.astype(o_ref.dtype)


def _nbr_attention(rpb, q, k_s, v_s, cache_k, cache_v, y_att, layer, n_heads, hd):
    n_dec, _, past, _ = cache_k.shape
    t = k_s.shape[0] // n_dec
    rows = t // GRID_W
    kr = min(WIN_ROWS, rows)
    assert kr % 2 == 0 and t % GRID_W == 0
    blk0 = (q.shape[0] - k_s.shape[0]) // t
    tok = pl.BlockSpec((t, hd), lambda h, b: (b, h))
    tok_q = pl.BlockSpec((t, hd), lambda h, b: (blk0 + b, h))
    cache = pl.BlockSpec((None, None, past, hd), lambda h, b: (b, layer, 0, h))
    return pl.pallas_call(
        functools.partial(_na_kernel, layer=layer, n_heads=n_heads, hd=hd, rows=rows, kr=kr),
        grid=(n_heads, n_dec),
        in_specs=[pl.BlockSpec(memory_space=pltpu.SMEM), tok_q, tok, tok, cache, cache,
                  pl.BlockSpec(memory_space=pl.ANY)],
        out_specs=tok_q,
        out_shape=jax.ShapeDtypeStruct(y_att.shape, y_att.dtype),
        scratch_shapes=[pltpu.VMEM((2 * WIN_ROWS - 2, GRID_W, 2 * GRID_W), F32)],
        input_output_aliases={6: 0},
        compiler_params=_params("arbitrary", "arbitrary"),
        name="nbr_attention",
    )(rpb.reshape(-1), q, k_s, v_s, cache_k, cache_v, y_att)


def _merge_kernel(ya_ref, yc_ref, ga_ref, gb_ref, wa_ref, wc_ref, o_ref, was_ref, wcs_ref):
    @pl.when(pl.program_id(1) == 0)
    def _():
        was_ref[...] = wa_ref[...].astype(BF16)
        wcs_ref[...] = wc_ref[...].astype(BF16)

    a = jnp.dot(ya_ref[...], was_ref[...], preferred_element_type=F32)
    c = jnp.dot(yc_ref[...], wcs_ref[...], preferred_element_type=F32)
    o = jax.nn.sigmoid(ga_ref[...].astype(F32)) * a + jax.nn.sigmoid(gb_ref[...].astype(F32)) * c
    o_ref[...] = o.astype(o_ref.dtype)


def _merge(y_att, y_conv, rest, w_br_att, w_br_conv, layer, gate_col0, d):
    r, a_dim = y_att.shape
    c_dim = y_conv.shape[1]
    tm, tn = ROW_TILE, COL_TILE
    gcb = gate_col0 // tn
    ndb = d // tn
    return pl.pallas_call(
        _merge_kernel,
        grid=(ndb, r // tm),
        in_specs=[pl.BlockSpec((tm, a_dim), lambda j, i: (i, 0)),
                  pl.BlockSpec((tm, c_dim), lambda j, i: (i, 0)),
                  pl.BlockSpec((tm, tn), lambda j, i: (i, gcb + j)),
                  pl.BlockSpec((tm, tn), lambda j, i: (i, gcb + ndb + j)),
                  pl.BlockSpec((None, a_dim, tn), lambda j, i: (layer, 0, j)),
                  pl.BlockSpec((None, c_dim, tn), lambda j, i: (layer, 0, j))],
        out_specs=pl.BlockSpec((tm, tn), lambda j, i: (i, j)),
        out_shape=jax.ShapeDtypeStruct((r, d), BF16),
        scratch_shapes=[pltpu.VMEM((a_dim, tn), BF16), pltpu.VMEM((c_dim, tn), BF16)],
        compiler_params=_params("arbitrary", "arbitrary"),
        name="merge",
    )(y_att, y_conv, rest, rest, w_br_att, w_br_conv)


def _outproj_kernel(m_ref, w_ref, xp_ref, xs_ref, gate_ref, o_ref, ws_ref, *, n_p, dec_seq):
    i = pl.program_id(1)

    @pl.when(i == 0)
    def _():
        ws_ref[...] = w_ref[...].astype(BF16)

    gate = gate_ref[pl.ds(_mod_row(i, n_p, o_ref.shape[0], dec_seq), 1), :]
    upd = gate * jnp.dot(m_ref[...], ws_ref[...], preferred_element_type=F32)

    @pl.when(i < n_p)
    def _():
        o_ref[...] = xp_ref[...] + upd

    @pl.when(i >= n_p)
    def _():
        o_ref[...] = xs_ref[...] + upd


def _outproj(merged, w_out, xp, xs, mod, layer, dec_seq):
    r, d = merged.shape
    tm, tn = ROW_TILE, COL_TILE
    n_p = xp.shape[0] // tm
    mrows = mod.shape[0]
    ndb = d // tn
    return pl.pallas_call(
        functools.partial(_outproj_kernel, n_p=n_p, dec_seq=dec_seq),
        grid=(ndb, r // tm),
        in_specs=[pl.BlockSpec((tm, d), lambda j, i: (i, 0)),
                  pl.BlockSpec((None, d, tn), lambda j, i: (layer, 0, j)),
                  pl.BlockSpec((tm, tn), lambda j, i: (jnp.minimum(i, n_p - 1), j)),
                  pl.BlockSpec((tm, tn), lambda j, i: (jnp.maximum(i - n_p, 0), j)),
                  pl.BlockSpec((mrows, tn), lambda j, i: (0, j))],
        out_specs=pl.BlockSpec((tm, tn), lambda j, i: (i, j)),
        out_shape=jax.ShapeDtypeStruct((r, d), F32),
        scratch_shapes=[pltpu.VMEM((d, tn), BF16)],
        compiler_params=_params("arbitrary", "arbitrary"),
        name="outproj",
    )(merged, w_out, xp, xs, mod)


def _split_bf16(x):
    hi = x.astype(BF16)
    return hi, (x - hi.astype(F32)).astype(BF16)


def _norm2_kernel(x_ref, sh_ref, sc_ref, g_ref, wr_ref, h_ref, lg_ref, w2_ref, *, n_p, dec_seq):
    i = pl.program_id(0)
    ep = lg_ref.shape[1]

    @pl.when(i == 0)
    def _():
        w_hi, w_lo = _split_bf16(wr_ref[...])
        w2_ref[:, :ep] = w_hi
        w2_ref[:, ep:] = w_lo

    r = _mod_row(i, n_p, h_ref.shape[0], dec_seq)
    y = _rmsnorm_f32(x_ref[...], g_ref[...])
    h = y * (1.0 + sc_ref[pl.ds(r, 1), :]) + sh_ref[pl.ds(r, 1), :]
    h_hi, h_lo = _split_bf16(h)
    h_ref[...] = h_hi
    p_hi = jnp.dot(h_hi, w2_ref[...], preferred_element_type=F32)
    p_lo = jnp.dot(h_lo, w2_ref[...], preferred_element_type=F32)
    lg_ref[...] = p_hi[:, :ep] + (p_hi[:, ep:] + (p_lo[:, :ep] + p_lo[:, ep:]))


def _norm2(x1, mod, g, w_router_pad, layer, n_p_rows, dec_seq):
    r, d = x1.shape
    tr = NORM_ROWS
    mrows = mod.shape[0]
    ep = w_router_pad.shape[-1]
    return pl.pallas_call(
        functools.partial(_norm2_kernel, n_p=n_p_rows // tr, dec_seq=dec_seq),
        grid=(r // tr,),
        in_specs=[pl.BlockSpec((tr, d), lambda i: (i, 0)),
                  pl.BlockSpec((mrows, d), lambda i: (0, 1)),
                  pl.BlockSpec((mrows, d), lambda i: (0, 2)),
                  pl.BlockSpec((None, 1, d), lambda i: (layer, 0, 0)),
                  pl.BlockSpec((None, d, ep), lambda i: (layer, 0, 0))],
        out_specs=[pl.BlockSpec((tr, d), lambda i: (i, 0)),
                   pl.BlockSpec((tr, ep), lambda i: (i, 0))],
        out_shape=[jax.ShapeDtypeStruct((r, d), BF16), jax.ShapeDtypeStruct((r, ep), F32)],
        scratch_shapes=[pltpu.VMEM((d, 2 * ep), BF16)],
        compiler_params=_params("arbitrary"),
        name="norm2_router",
    )(x1, mod, mod, g.reshape(g.shape[0], 1, d), w_router_pad)


def _route_kernel(lg_ref, h_ref, *refs, n, cap, n_exp, aliased):
    xs_ref, st_ref, g_ref, s_scr, s32_scr, afft_scr = refs[2 if aliased else 0:]
    ep = lg_ref.shape[1]

    @pl.when(pl.program_id(1) == 0)
    def _():
        lane = lax.broadcasted_iota(I32, (n, ep), 1)
        lg = jnp.where(lane < n_exp, lg_ref[...], -jnp.inf)
        ex = jnp.exp(lg - jnp.max(lg, axis=-1, keepdims=True))
        aff = ex / jnp.sum(ex, axis=-1, keepdims=True)
        afft_scr[...] = aff.T
        slot_id = lax.broadcasted_iota(I32, (cap, n), 0).astype(F32)

        def per_expert(e, carry):
            row = afft_scr[pl.ds(e, 1), :]
            col = jnp.sum(jnp.where(lane == e, aff, 0.0), axis=1, keepdims=True)
            earlier = lax.broadcasted_iota(I32, (n, n), 0) < lax.broadcasted_iota(I32, (n, n), 1)
            beats = jnp.where(earlier, jnp.where(col >= row, 1.0, 0.0), jnp.where(col > row, 1.0, 0.0))
            rank = jnp.sum(beats, axis=0, keepdims=True)
            s_e = (rank == slot_id).astype(F32)
            s32_scr[pl.ds(pl.multiple_of(e * cap, cap), cap), :] = s_e
            g_ref[e] = jnp.sum(s_e * row, axis=1, keepdims=True)
            return carry

        lax.fori_loop(0, n_exp, per_expert, 0, unroll=max(1, 1024 // n))
        s32 = s32_scr[...]
        s_scr[...] = s32.astype(BF16)
        st_ref[...] = s32.T.astype(st_ref.dtype)

    x = jnp.dot(s_scr[...], h_ref[...], preferred_element_type=F32)
    xs_ref[...] = x.astype(xs_ref.dtype).reshape(xs_ref.shape)


def _route(logits, h2, row0, n_req, n, n_exp, total_slots, prev):
    d = h2.shape[1]
    cap = CAPACITY_FACTOR * n // n_exp
    ecap = n_exp * cap
    ep = logits.shape[1]
    td = d if n * d <= 1024 * 1024 else 1024
    rb0 = row0 // n
    sb0 = 0 if prev is None else prev[2]
    ins = [logits, h2]
    in_specs = [pl.BlockSpec((n, ep), lambda b, t: (rb0 + b, 0)),
                pl.BlockSpec((n, td), lambda b, t: (rb0 + b, t))]
    aliases = {}
    if prev is not None:
        ins += [prev[0], prev[1]]
        in_specs += [pl.BlockSpec(memory_space=pl.ANY)] * 2
        aliases = {2: 0, 3: 2}
    xs, st, g = pl.pallas_call(
        functools.partial(_route_kernel, n=n, cap=cap, n_exp=n_exp, aliased=prev is not None),
        grid=(n_req, d // td),
        in_specs=in_specs,
        out_specs=[pl.BlockSpec((n_exp, cap, td), lambda b, t: (0, sb0 + b, t)),
                   pl.BlockSpec((n, ecap), lambda b, t: (b, 0)),
                   pl.BlockSpec((n_exp, cap, 1), lambda b, t: (0, sb0 + b, 0))],
        out_shape=[jax.ShapeDtypeStruct((n_exp, total_slots, d), BF16),
                   jax.ShapeDtypeStruct((n_req * n, ecap), BF16),
                   jax.ShapeDtypeStruct((n_exp, total_slots, 1), F32)],
        scratch_shapes=[pltpu.VMEM((ecap, n), BF16), pltpu.VMEM((ecap, n), F32), pltpu.VMEM((ep, n), F32)],
        input_output_aliases=aliases,
        compiler_params=_params("arbitrary", "arbitrary"),
        name="route_gather_n%d" % n,
    )(*ins)
    return xs, st, g


def _ffn_up_kernel(x_ref, wg_ref, wu_ref, o_ref):
    x = x_ref[...]
    a = jnp.dot(x, wg_ref[...].astype(BF16), preferred_element_type=F32)
    u = jnp.dot(x, wu_ref[...].astype(BF16), preferred_element_type=F32)
    o_ref[...] = (jax.nn.silu(a) * u).astype(o_ref.dtype)


def _ffn_up(xs, w_gate, w_up, layer):
    n_exp, slots, d = xs.shape
    f = w_gate.shape[-1]
    tf = 256
    wspec = pl.BlockSpec((None, None, d, tf), lambda e, j: (layer, e, 0, j))
    return pl.pallas_call(
        _ffn_up_kernel,
        grid=(n_exp, f // tf),
        in_specs=[pl.BlockSpec((None, slots, d), lambda e, j: (e, 0, 0)), wspec, wspec],
        out_specs=pl.BlockSpec((None, slots, tf), lambda e, j: (e, 0, j)),
        out_shape=jax.ShapeDtypeStruct((n_exp, slots, f), BF16),
        compiler_params=_params("arbitrary", "arbitrary"),
        name="ffn_up",
    )(xs, w_gate, w_up)


def _ffn_down_kernel(h_ref, w_ref, g_ref, o_ref):
    o = jnp.dot(h_ref[...], w_ref[...].astype(BF16), preferred_element_type=F32)
    o_ref[...] = (o * g_ref[...]).astype(o_ref.dtype)


def _ffn_down(hmid, w_down, g, layer):
    n_exp, slots, f = hmid.shape
    d = w_down.shape[-1]
    tn = 2 * COL_TILE
    return pl.pallas_call(
        _ffn_down_kernel,
        grid=(n_exp, d // tn),
        in_specs=[pl.BlockSpec((None, slots, f), lambda e, j: (e, 0, 0)),
                  pl.BlockSpec((None, None, f, tn), lambda e, j: (layer, e, 0, j)),
                  pl.BlockSpec((None, slots, 1), lambda e, j: (e, 0, 0))],
        out_specs=pl.BlockSpec((None, slots, tn), lambda e, j: (e, 0, j)),
        out_shape=jax.ShapeDtypeStruct((n_exp, slots, d), BF16),
        compiler_params=_params("arbitrary", "arbitrary"),
        name="ffn_down",
    )(hmid, w_down, g)


def _combine_kernel(x_ref, st_ref, o_in_ref, gate_ref, fg_ref, o_ref, *, n_k, per_req_mod, final):
    kk = pl.program_id(2)
    mod_row = 1 + pl.program_id(0) if per_req_mod else 0
    part = jnp.dot(st_ref[...], o_in_ref[...].reshape(st_ref.shape[1], o_ref.shape[1]),
                   preferred_element_type=F32)

    @pl.when(kk == 0)
    def _():
        o_ref[...] = part

    @pl.when(kk > 0)
    def _():
        o_ref[...] += part

    @pl.when(kk == n_k - 1)
    def _():
        x2 = x_ref[...] + gate_ref[pl.ds(mod_row, 1), :] * o_ref[...]
        o_ref[...] = _rmsnorm_f32(x2, fg_ref[...]) if final else x2


def _combine(x1, st, o_exp, mod, final_g, row0, n_req, n, slot_blk0, per_req_mod, final):
    d = x1.shape[1]
    n_exp = o_exp.shape[0]
    ecap = st.shape[1]
    cap = ecap // n_exp
    tr = min(n, NORM_ROWS)
    e_chunk = max(1, min(n_exp, 512 // cap))
    n_k = n_exp // e_chunk
    rb0 = row0 // tr
    rt = n // tr
    mrows = mod.shape[0]
    return pl.pallas_call(
        functools.partial(_combine_kernel, n_k=n_k, per_req_mod=per_req_mod, final=final),
        grid=(n_req, rt, n_k),
        in_specs=[pl.BlockSpec((tr, d), lambda b, t, k: (rb0 + b * rt + t, 0)),
                  pl.BlockSpec((tr, e_chunk * cap), lambda b, t, k: (b * rt + t, k)),
                  pl.BlockSpec((e_chunk, cap, d), lambda b, t, k: (k, slot_blk0 + b, 0)),
                  pl.BlockSpec((mrows, d), lambda b, t, k: (0, 3)),
                  pl.BlockSpec((1, d), lambda b, t, k: (0, 0))],
        out_specs=pl.BlockSpec((tr, d), lambda b, t, k: (b * rt + t, 0)),
        out_shape=jax.ShapeDtypeStruct((n_req * n, d), F32),
        compiler_params=_params("arbitrary", "arbitrary", "arbitrary"),
        name="combine_n%d" % n,
    )(x1, st, o_exp, mod, final_g.reshape(1, d))


def kernel(x_prompt, x_sample, cache_k, cache_v, c, c_ctx, w_ada, b_ada, norm1_g, norm2_g, w_in, conv_w, rpb, w_br_att, w_br_conv, w_out, w_router, w_gate, w_up, w_down, final_g):
    n_req, seq, d = x_prompt.shape
    n_dec, dec_seq, _ = x_sample.shape
    depth = w_in.shape[0]
    n_heads, hd = cache_k.shape[-2:]
    a_dim = n_heads * hd
    c_dim = conv_w.shape[-1]
    n_exp = w_router.shape[-1]
    rp, rs = n_req * seq, n_dec * dec_seq
    tm = ROW_TILE
    assert rp % tm == 0 and rs % tm == 0 and dec_seq % NORM_ROWS == 0 and seq % NORM_ROWS == 0

    xp = x_prompt.reshape(rp, d)
    xs = x_sample.reshape(rs, d)
    ck = cache_k.reshape(cache_k.shape[:3] + (a_dim,))
    cv = cache_v.reshape(cache_v.shape[:3] + (a_dim,))
    cond = jnp.concatenate([c_ctx[None, :], c], axis=0)
    cond = jnp.pad(cond, ((0, -cond.shape[0] % SUBLANES), (0, 0)))
    w_router_pad = jnp.pad(w_router, ((0, 0), (0, 0), (0, -n_exp % LANES)))
    cap_p = CAPACITY_FACTOR * seq // n_exp
    cap_s = CAPACITY_FACTOR * dec_seq // n_exp
    slots = n_req * cap_p + n_dec * cap_s

    new_k, new_v = [], []
    for l in range(depth):
        last = l == depth - 1
        mod1 = _adaln(cond, w_ada, b_ada, l, 2 * d)
        h = _norm1(xp, xs, mod1, norm1_g, l, dec_seq)
        q = _proj(h, w_in, l, 0, a_dim, BF16, "proj_q")
        k_p, k_s = _proj_split(h, w_in, l, a_dim, a_dim, rp, "proj_k")
        v_p, v_s = _proj_split(h, w_in, l, 2 * a_dim, a_dim, rp, "proj_v")
        y_conv = _conv_proj(h, w_in, conv_w, l, 3 * a_dim, c_dim, rp, seq, dec_seq)
        gates, mod = _proj_ada(h, w_in, l, 3 * a_dim + 3 * c_dim, 2 * d, cond.T, 1 + n_dec,
                               w_ada, b_ada, 2 * d, "proj_gates")
        new_k.append(k_p.reshape(n_req, seq, n_heads, hd))
        new_v.append(v_p.reshape(n_req, seq, n_heads, hd))

        y_att = _ctx_attention(q, k_p, v_p, n_req, seq, n_heads, hd)
        y_att = _nbr_attention(rpb, q, k_s, v_s, ck, cv, y_att, l, n_heads, hd)
        merged = _merge(y_att, y_conv, gates, w_br_att, w_br_conv, l, 0, d)
        x1 = _outproj(merged, w_out, xp, xs, mod, l, dec_seq)

        h2, logits = _norm2(x1, mod, norm2_g, w_router_pad, l, rp, dec_seq)
        xg, st_p, g = _route(logits, h2, 0, n_req, seq, n_exp, slots, None)
        xg, st_s, g = _route(logits, h2, rp, n_dec, dec_seq, n_exp, slots,
                             (xg, g, n_req * cap_p // cap_s))
        o_exp = _ffn_down(_ffn_up(xg, w_gate, w_up, l), w_down, g, l)
        xp = _combine(x1, st_p, o_exp, mod, final_g, 0, n_req, seq, 0, False, last)
        xs = _combine(x1, st_s, o_exp, mod, final_g, rp, n_dec, dec_seq,
                      n_req * cap_p // cap_s, True, last)

    y_prompt = xp.reshape(n_req, seq, d)
    y_sample = xs.reshape(n_dec, dec_seq, d)
    return (y_prompt, y_sample, jnp.stack(new_k, axis=1), jnp.stack(new_v, axis=1))
```

```python
import functools

import jax
import jax.numpy as jnp
from jax import lax
from jax.experimental import pallas as pl
from jax.experimental.pallas import tpu as pltpu

BF16 = jnp.bfloat16
F32 = jnp.float32
I32 = jnp.int32

GRID_W = 64
WIN_ROWS = 8
WIN_COLS = 16
CAPACITY_FACTOR = 2
N_MOD = 6
RMS_EPS = 1e-6

LANES = 128
SUBLANES = 8
VMEM_LIMIT_BYTES = 56 * 1024 * 1024

ROW_TILE = 1024
COL_TILE = 512
CONV_COL_TILE = 256
FFN_COL_TILE = 256
ROUTE_TILE_ELEMS = 1024 * 1024
NORM_ROWS = 256
ROW_CHUNK = 16
CHUNK_UNROLL = 4

_NT = (((1,), (1,)), ((), ()))


def _params(*sem):
    return pltpu.CompilerParams(dimension_semantics=sem, vmem_limit_bytes=VMEM_LIMIT_BYTES)


def _mod_row(tile, n_prompt_tiles, tile_rows, dec_seq):
    req = lax.div(jnp.maximum(tile - n_prompt_tiles, 0) * tile_rows, dec_seq)
    return jnp.where(tile < n_prompt_tiles, 0, 1 + req)


def _rmsnorm_f32(x, g):
    return x * lax.rsqrt(jnp.mean(x * x, axis=-1, keepdims=True) + RMS_EPS) * g


def _for_row_chunks(n_rows, fn):
    def body(c, carry):
        fn(pl.ds(pl.multiple_of(c * ROW_CHUNK, ROW_CHUNK), ROW_CHUNK))
        return carry

    lax.fori_loop(0, n_rows // ROW_CHUNK, body, 0, unroll=CHUNK_UNROLL)


def _adaln_kernel(c_ref, w_ref, b_ref, o_ref):
    s = jax.nn.silu(c_ref[...])
    o_ref[...] = jnp.dot(s.astype(BF16), w_ref[...].astype(BF16), preferred_element_type=F32) + b_ref[...]


def _adaln(cond, w_ada, b_ada, layer, n):
    rows, d = cond.shape
    tn = COL_TILE
    return pl.pallas_call(
        _adaln_kernel,
        grid=(n // tn,),
        in_specs=[pl.BlockSpec((rows, d), lambda j: (0, 0)),
                  pl.BlockSpec((None, d, tn), lambda j: (layer, 0, j)),
                  pl.BlockSpec((None, 1, tn), lambda j: (layer, 0, j))],
        out_specs=pl.BlockSpec((rows, tn), lambda j: (0, j)),
        out_shape=jax.ShapeDtypeStruct((rows, n), F32),
        compiler_params=_params("arbitrary"),
        name="adaln",
    )(cond, w_ada, b_ada.reshape(b_ada.shape[0], 1, -1))


def _norm1_kernel(xp_ref, xs_ref, sh_ref, sc_ref, g_ref, o_ref, *, n_p, dec_seq):
    i = pl.program_id(0)
    r = _mod_row(i, n_p, o_ref.shape[0], dec_seq)
    shift = sh_ref[pl.ds(r, 1), :]
    scale = sc_ref[pl.ds(r, 1), :]

    def emit(x_ref):
        def chunk(rs):
            y = _rmsnorm_f32(x_ref[rs, :], g_ref[...])
            o_ref[rs, :] = (y * (1.0 + scale) + shift).astype(o_ref.dtype)

        _for_row_chunks(o_ref.shape[0], chunk)

    pl.when(i < n_p)(lambda: emit(xp_ref))
    pl.when(i >= n_p)(lambda: emit(xs_ref))


def _norm1(xp, xs, mod, g, layer, dec_seq):
    rp, d = xp.shape
    rs = xs.shape[0]
    tr = 2 * NORM_ROWS
    assert dec_seq % tr == 0 and rp % tr == 0
    n_p, n_s = rp // tr, rs // tr
    mrows = mod.shape[0]
    return pl.pallas_call(
        functools.partial(_norm1_kernel, n_p=n_p, dec_seq=dec_seq),
        grid=(n_p + n_s,),
        in_specs=[pl.BlockSpec((tr, d), lambda i: (jnp.minimum(i, n_p - 1), 0)),
                  pl.BlockSpec((tr, d), lambda i: (jnp.maximum(i - n_p, 0), 0)),
                  pl.BlockSpec((mrows, d), lambda i: (0, 0)),
                  pl.BlockSpec((mrows, d), lambda i: (0, 1)),
                  pl.BlockSpec((None, 1, d), lambda i: (layer, 0, 0))],
        out_specs=pl.BlockSpec((tr, d), lambda i: (i, 0)),
        out_shape=jax.ShapeDtypeStruct((rp + rs, d), BF16),
        compiler_params=_params("arbitrary"),
        name="norm1",
    )(xp, xs, mod, mod, g.reshape(g.shape[0], 1, d))


def _proj_kernel(a_ref, b_ref, o_ref, bs_ref):
    @pl.when(pl.program_id(1) == 0)
    def _():
        bs_ref[...] = b_ref[...].astype(BF16)

    o_ref[...] = jnp.dot(a_ref[...], bs_ref[...], preferred_element_type=F32).astype(o_ref.dtype)


def _proj(a, w, layer, col0, ncols, out_dtype, name):
    m, k = a.shape
    tm, tn = ROW_TILE, COL_TILE
    cb0 = col0 // tn
    return pl.pallas_call(
        _proj_kernel,
        grid=(ncols // tn, m // tm),
        in_specs=[pl.BlockSpec((tm, k), lambda j, i: (i, 0)),
                  pl.BlockSpec((None, k, tn), lambda j, i: (layer, 0, cb0 + j))],
        out_specs=pl.BlockSpec((tm, tn), lambda j, i: (i, j)),
        out_shape=jax.ShapeDtypeStruct((m, ncols), out_dtype),
        scratch_shapes=[pltpu.VMEM((k, tn), BF16)],
        compiler_params=_params("arbitrary", "arbitrary"),
        name=name,
    )(a, w)


def _proj_ada_kernel(a_ref, b_ref, ct_ref, wa_ref, ba_ref, o_ref, mod_ref, bs_ref, sb_ref, *, n_cond):
    j, i = pl.program_id(0), pl.program_id(1)

    @pl.when(i == 0)
    def _():
        bs_ref[...] = b_ref[...].astype(BF16)

    @pl.when((i == 0) & (j == 0))
    def _():
        s_t = jax.nn.silu(ct_ref[...])
        for r in range(n_cond):
            sb_ref[r] = jnp.broadcast_to(s_t[:, r:r + 1], sb_ref.shape[1:])

    o_ref[...] = jnp.dot(a_ref[...], bs_ref[...], preferred_element_type=F32).astype(o_ref.dtype)
    w = wa_ref[...]
    sub = lax.broadcasted_iota(I32, mod_ref.shape, 0)
    acc = jnp.zeros(mod_ref.shape, F32)
    for r in range(n_cond):
        row = jnp.sum(sb_ref[r] * w, axis=0, keepdims=True)
        acc = jnp.where(sub == r, row, acc)
    mod_ref[...] = acc + ba_ref[...]


def _proj_ada(a, w, layer, col0, ncols, cond_t, n_cond, w_ada, b_ada, ada_col0, name):
    m, k = a.shape
    d, crow = cond_t.shape
    tm, tn = ROW_TILE, COL_TILE
    cb0 = col0 // tn
    n_i = m // tm
    n_ada = w_ada.shape[-1] - ada_col0
    n_blocks = n_ada // LANES
    assert n_blocks <= (ncols // tn) * n_i and ada_col0 % LANES == 0
    ab0 = ada_col0 // LANES

    def ada_blk(j, i):
        return ab0 + jnp.minimum(j * n_i + i, n_blocks - 1)

    return pl.pallas_call(
        functools.partial(_proj_ada_kernel, n_cond=n_cond),
        grid=(ncols // tn, n_i),
        in_specs=[pl.BlockSpec((tm, k), lambda j, i: (i, 0)),
                  pl.BlockSpec((None, k, tn), lambda j, i: (layer, 0, cb0 + j)),
                  pl.BlockSpec((d, crow), lambda j, i: (0, 0)),
                  pl.BlockSpec((None, d, LANES), lambda j, i: (layer, 0, ada_blk(j, i))),
                  pl.BlockSpec((None, 1, LANES), lambda j, i: (layer, 0, ada_blk(j, i)))],
        out_specs=[pl.BlockSpec((tm, tn), lambda j, i: (i, j)),
                   pl.BlockSpec((crow, LANES), lambda j, i: (0, ada_blk(j, i) - ab0))],
        out_shape=[jax.ShapeDtypeStruct((m, ncols), BF16),
                   jax.ShapeDtypeStruct((crow, n_ada), F32)],
        scratch_shapes=[pltpu.VMEM((k, tn), BF16), pltpu.VMEM((n_cond, d, LANES), F32)],
        compiler_params=_params("arbitrary", "arbitrary"),
        name=name,
    )(a, w, cond_t, w_ada, b_ada.reshape(b_ada.shape[0], 1, -1))


def _proj_split_kernel(a_ref, b_ref, op_ref, os_ref, bs_ref, *, n_p):
    i = pl.program_id(1)

    @pl.when(i == 0)
    def _():
        bs_ref[...] = b_ref[...].astype(BF16)

    res = jnp.dot(a_ref[...], bs_ref[...], preferred_element_type=F32)

    @pl.when(i < n_p)
    def _():
        op_ref[...] = res

    @pl.when(i >= n_p)
    def _():
        os_ref[...] = res


def _proj_split(a, w, layer, col0, ncols, n_prompt_rows, name):
    m, k = a.shape
    tm, tn = ROW_TILE, COL_TILE
    cb0 = col0 // tn
    n_p = n_prompt_rows // tm
    return pl.pallas_call(
        functools.partial(_proj_split_kernel, n_p=n_p),
        grid=(ncols // tn, m // tm),
        in_specs=[pl.BlockSpec((tm, k), lambda j, i: (i, 0)),
                  pl.BlockSpec((None, k, tn), lambda j, i: (layer, 0, cb0 + j))],
        out_specs=[pl.BlockSpec((tm, tn), lambda j, i: (jnp.minimum(i, n_p - 1), j)),
                   pl.BlockSpec((tm, tn), lambda j, i: (jnp.maximum(i - n_p, 0), j))],
        out_shape=[jax.ShapeDtypeStruct((n_prompt_rows, ncols), F32),
                   jax.ShapeDtypeStruct((m - n_prompt_rows, ncols), F32)],
        scratch_shapes=[pltpu.VMEM((k, tn), BF16)],
        compiler_params=_params("arbitrary", "arbitrary"),
        name=name,
    )(a, w)


def _seq_pos(rows, seq):
    t = lax.broadcasted_iota(I32, (rows, 1), 0)
    return t & (seq - 1) if seq & (seq - 1) == 0 else lax.rem(t, seq)


def _conv_proj_kernel(a_ref, wu_ref, wb_ref, wc_ref, cw_ref, o_ref, us_ref, bs_ref, cs_ref, *, n_p, seq_p, seq_s):
    i = pl.program_id(1)
    tm = o_ref.shape[0]

    @pl.when(i == 0)
    def _():
        us_ref[...] = wu_ref[...].astype(BF16)
        bs_ref[...] = wb_ref[...].astype(BF16)
        cs_ref[...] = wc_ref[...].astype(BF16)

    a = a_ref[...]
    dot = functools.partial(jnp.dot, preferred_element_type=F32)
    z = dot(a, cs_ref[...]) * dot(a, us_ref[...])
    prompt = i < n_p
    pos = jnp.where(prompt, _seq_pos(tm, seq_p), _seq_pos(tm, seq_s))
    last = jnp.where(prompt, seq_p - 1, seq_s - 1)
    z_prev = jnp.where(pos == 0, 0.0, pltpu.roll(z, 1, 0))
    z_next = jnp.where(pos == last, 0.0, pltpu.roll(z, tm - 1, 0))
    w = cw_ref[...]
    y = z_prev * w[0:1, :] + z * w[1:2, :] + z_next * w[2:3, :]
    o_ref[...] = (dot(a, bs_ref[...]) * y).astype(o_ref.dtype)


def _conv_proj(h, w_in, conv_w, layer, col0, c_dim, n_p_rows, seq_p, seq_s):
    r, k = h.shape
    tm, tn = ROW_TILE, CONV_COL_TILE
    assert tm % seq_p == 0 and tm % seq_s == 0, "row tile must hold whole sequences"
    assert conv_w.shape[1] == 3, "centred width-3 convolution"
    cb0, ncb = col0 // tn, c_dim // tn

    def wspec(part):
        return pl.BlockSpec((None, k, tn), lambda j, i: (layer, 0, cb0 + part * ncb + j))

    return pl.pallas_call(
        functools.partial(_conv_proj_kernel, n_p=n_p_rows // tm, seq_p=seq_p, seq_s=seq_s),
        grid=(ncb, r // tm),
        in_specs=[pl.BlockSpec((tm, k), lambda j, i: (i, 0)), wspec(0), wspec(1), wspec(2),
                  pl.BlockSpec((None, 3, tn), lambda j, i: (layer, 0, j))],
        out_specs=pl.BlockSpec((tm, tn), lambda j, i: (i, j)),
        out_shape=jax.ShapeDtypeStruct((r, c_dim), BF16),
        scratch_shapes=[pltpu.VMEM((k, tn), BF16)] * 3,
        compiler_params=_params("arbitrary", "arbitrary"),
        name="proj_conv",
    )(h, w_in, w_in, w_in, conv_w)


def _softmax_rows(parts):
    m = functools.reduce(jnp.maximum, [jnp.max(s, axis=-1, keepdims=True) for s in parts])
    es = [jnp.exp(s - m) for s in parts]
    inv = 1.0 / functools.reduce(jnp.add, [jnp.sum(e, axis=-1, keepdims=True) for e in es])
    return [e * inv for e in es]


def _ctx_attn_kernel(q_ref, k_ref, v_ref, o_ref, *, n_heads, hd):
    scale = hd ** -0.5
    for h in range(n_heads):
        sl = slice(h * hd, (h + 1) * hd)
        s = lax.dot_general(q_ref[:, sl], k_ref[:, sl].astype(BF16), _NT, preferred_element_type=F32) * scale
        e = jnp.exp(s - jnp.max(s, axis=-1, keepdims=True)).astype(BF16)
        v = v_ref[:, sl].astype(BF16)
        oa = jnp.dot(e, jnp.concatenate([v, jnp.ones_like(v)], axis=1), preferred_element_type=F32)
        o_ref[:, sl] = (oa[:, :hd] / oa[:, hd:]).astype(o_ref.dtype)


def _ctx_attention(q, k_p, v_p, n_req, seq, n_heads, hd):
    a = q.shape[1]
    return pl.pallas_call(
        functools.partial(_ctx_attn_kernel, n_heads=n_heads, hd=hd),
        grid=(n_req,),
        in_specs=[pl.BlockSpec((seq, a), lambda b: (b, 0))] * 3,
        out_specs=pl.BlockSpec((seq, a), lambda b: (b, 0)),
        out_shape=jax.ShapeDtypeStruct((n_req * seq, a), BF16),
        compiler_params=_params("arbitrary"),
        name="ctx_attention",
    )(q, k_p, v_p)


def _na_kernel(rpb_ref, q_ref, k_ref, v_ref, ck_ref, cv_ref, o_ref, pairs_ref, *, layer, n_heads, hd, rows, kr):
    w, wr, wc = GRID_W, WIN_ROWS, WIN_COLS
    ndr, ndc = 2 * wr - 1, 2 * wc - 1
    scale = hd ** -0.5

    @pl.when(pl.program_id(1) == 0)
    def _():
        base = (layer * n_heads + pl.program_id(0)) * (ndr * ndc)
        qc = lax.broadcasted_iota(I32, (w, 2 * w), 0)
        lane = lax.broadcasted_iota(I32, (w, 2 * w), 1)
        second = lane >= w
        kc = jnp.where(second, lane - w, lane)
        c0 = jnp.clip(qc - wc // 2, 0, w - wc)
        in_win = (kc >= c0) & (kc < c0 + wc)
        dc_idx = jnp.clip(kc - qc + wc - 1, 0, ndc - 1)
        dc_masks = [dc_idx == dc for dc in range(ndc)]
        for dr in range(ndr - 1):
            t = jnp.zeros((w, 2 * w), F32)
            for dc in range(ndc):
                va = rpb_ref[base + dr * ndc + dc]
                vb = rpb_ref[base + (dr + 1) * ndc + dc]
                t = jnp.where(dc_masks[dc], jnp.where(second, vb, va), t)
            pairs_ref[dr] = jnp.where(in_win, t, -jnp.inf)

    q = q_ref[...]
    kb = k_ref[...].astype(BF16)
    vb16 = v_ref[...].astype(BF16)
    cvb = cv_ref[...].astype(BF16)
    s_ctx = lax.dot_general(q, ck_ref[...].astype(BF16), _NT, preferred_element_type=F32) * scale

    groups = []
    for r in range(rows):
        r0 = min(max(r - kr // 2, 0), rows - kr)
        if groups and groups[-1][2] == r0:
            groups[-1][1] += 1
        else:
            groups.append([r, 1, r0])

    s_loc = []
    for r_first, cnt, r0 in groups:
        qs = slice(r_first * w, (r_first + cnt) * w)
        ks = slice(r0 * w, (r0 + kr) * w)
        bias = jnp.concatenate(
            [jnp.concatenate([pairs_ref[r0 - r + wr - 1 + 2 * m] for m in range(kr // 2)], axis=1)
             for r in range(r_first, r_first + cnt)], axis=0)
        s_loc.append(lax.dot_general(q[qs], kb[ks], _NT, preferred_element_type=F32) * scale + bias)
    p_loc, p_ctx = _softmax_rows([jnp.concatenate(s_loc, axis=0), s_ctx])
    p_loc = p_loc.astype(BF16)
    o_ctx = jnp.dot(p_ctx.astype(BF16), cvb, preferred_element_type=F32)
    for r_first, cnt, r0 in groups:
        qs = slice(r_first * w, (r_first + cnt) * w)
        ks = slice(r0 * w, (r0 + kr) * w)
        o = jnp.dot(p_loc[qs], vb16[ks], preferred_element_type=F32) + o_ctx[qs]
        o_ref[qs, :] = o.astype(o_ref.dtype)


def _nbr_attention(rpb, q, k_s, v_s, cache_k, cache_v, layer, n_heads, hd):
    n_dec, _, past, _ = cache_k.shape
    t = k_s.shape[0] // n_dec
    rows = t // GRID_W
    kr = min(WIN_ROWS, rows)
    assert kr % 2 == 0 and t % GRID_W == 0
    blk0 = (q.shape[0] - k_s.shape[0]) // t
    tok = pl.BlockSpec((t, hd), lambda h, b: (b, h))
    tok_q = pl.BlockSpec((t, hd), lambda h, b: (blk0 + b, h))
    cache = pl.BlockSpec((None, None, past, hd), lambda h, b: (b, layer, 0, h))
    return pl.pallas_call(
        functools.partial(_na_kernel, layer=layer, n_heads=n_heads, hd=hd, rows=rows, kr=kr),
        grid=(n_heads, n_dec),
        in_specs=[pl.BlockSpec(memory_space=pltpu.SMEM), tok_q, tok, tok, cache, cache],
        out_specs=tok,
        out_shape=jax.ShapeDtypeStruct(k_s.shape, BF16),
        scratch_shapes=[pltpu.VMEM((2 * WIN_ROWS - 2, GRID_W, 2 * GRID_W), F32)],
        compiler_params=_params("arbitrary", "arbitrary"),
        name="nbr_attention",
    )(rpb.reshape(-1), q, k_s, v_s, cache_k, cache_v)


def _merge_kernel(yap_ref, yas_ref, yc_ref, ga_ref, gb_ref, wa_ref, wc_ref, o_ref, was_ref, wcs_ref, *, n_p):
    i = pl.program_id(1)

    @pl.when(i == 0)
    def _():
        was_ref[...] = wa_ref[...].astype(BF16)
        wcs_ref[...] = wc_ref[...].astype(BF16)

    def emit(ya_ref):
        a = jnp.dot(ya_ref[...], was_ref[...], preferred_element_type=F32)
        c = jnp.dot(yc_ref[...], wcs_ref[...], preferred_element_type=F32)
        o = jax.nn.sigmoid(ga_ref[...].astype(F32)) * a + jax.nn.sigmoid(gb_ref[...].astype(F32)) * c
        o_ref[...] = o.astype(o_ref.dtype)

    pl.when(i < n_p)(lambda: emit(yap_ref))
    pl.when(i >= n_p)(lambda: emit(yas_ref))


def _merge(y_att_p, y_att_s, y_conv, rest, w_br_att, w_br_conv, layer, gate_col0, d):
    a_dim = y_att_p.shape[1]
    r, c_dim = y_conv.shape
    tm, tn = ROW_TILE, COL_TILE
    gcb = gate_col0 // tn
    ndb = d // tn
    n_p = y_att_p.shape[0] // tm
    return pl.pallas_call(
        functools.partial(_merge_kernel, n_p=n_p),
        grid=(ndb, r // tm),
        in_specs=[pl.BlockSpec((tm, a_dim), lambda j, i: (jnp.minimum(i, n_p - 1), 0)),
                  pl.BlockSpec((tm, a_dim), lambda j, i: (jnp.maximum(i - n_p, 0), 0)),
                  pl.BlockSpec((tm, c_dim), lambda j, i: (i, 0)),
                  pl.BlockSpec((tm, tn), lambda j, i: (i, gcb + j)),
                  pl.BlockSpec((tm, tn), lambda j, i: (i, gcb + ndb + j)),
                  pl.BlockSpec((None, a_dim, tn), lambda j, i: (layer, 0, j)),
                  pl.BlockSpec((None, c_dim, tn), lambda j, i: (layer, 0, j))],
        out_specs=pl.BlockSpec((tm, tn), lambda j, i: (i, j)),
        out_shape=jax.ShapeDtypeStruct((r, d), BF16),
        scratch_shapes=[pltpu.VMEM((a_dim, tn), BF16), pltpu.VMEM((c_dim, tn), BF16)],
        compiler_params=_params("arbitrary", "arbitrary"),
        name="merge",
    )(y_att_p, y_att_s, y_conv, rest, rest, w_br_att, w_br_conv)


def _outproj_kernel(m_ref, w_ref, xp_ref, xs_ref, gate_ref, o_ref, ws_ref, *, n_p, dec_seq):
    i = pl.program_id(1)

    @pl.when(i == 0)
    def _():
        ws_ref[...] = w_ref[...].astype(BF16)

    gate = gate_ref[pl.ds(_mod_row(i, n_p, o_ref.shape[0], dec_seq), 1), :]
    upd = gate * jnp.dot(m_ref[...], ws_ref[...], preferred_element_type=F32)

    @pl.when(i < n_p)
    def _():
        o_ref[...] = xp_ref[...] + upd

    @pl.when(i >= n_p)
    def _():
        o_ref[...] = xs_ref[...] + upd


def _outproj(merged, w_out, xp, xs, mod, layer, dec_seq):
    r, d = merged.shape
    tm, tn = ROW_TILE, COL_TILE
    n_p = xp.shape[0] // tm
    mrows = mod.shape[0]
    ndb = d // tn
    return pl.pallas_call(
        functools.partial(_outproj_kernel, n_p=n_p, dec_seq=dec_seq),
        grid=(ndb, r // tm),
        in_specs=[pl.BlockSpec((tm, d), lambda j, i: (i, 0)),
                  pl.BlockSpec((None, d, tn), lambda j, i: (layer, 0, j)),
                  pl.BlockSpec((tm, tn), lambda j, i: (jnp.minimum(i, n_p - 1), j)),
                  pl.BlockSpec((tm, tn), lambda j, i: (jnp.maximum(i - n_p, 0), j)),
                  pl.BlockSpec((mrows, tn), lambda j, i: (0, j))],
        out_specs=pl.BlockSpec((tm, tn), lambda j, i: (i, j)),
        out_shape=jax.ShapeDtypeStruct((r, d), F32),
        scratch_shapes=[pltpu.VMEM((d, tn), BF16)],
        compiler_params=_params("arbitrary", "arbitrary"),
        name="outproj",
    )(merged, w_out, xp, xs, mod)


def _split_bf16(x):
    hi = x.astype(BF16)
    return hi, (x - hi.astype(F32)).astype(BF16)


def _norm2_kernel(x_ref, sh_ref, sc_ref, g_ref, wr_ref, h_ref, lg_ref, w2_ref, *, n_p, dec_seq):
    i = pl.program_id(0)
    ep = lg_ref.shape[1]

    @pl.when(i == 0)
    def _():
        w_hi, w_lo = _split_bf16(wr_ref[...])
        w2_ref[:, :ep] = w_hi
        w2_ref[:, ep:] = w_lo

    r = _mod_row(i, n_p, h_ref.shape[0], dec_seq)
    y = _rmsnorm_f32(x_ref[...], g_ref[...])
    h = y * (1.0 + sc_ref[pl.ds(r, 1), :]) + sh_ref[pl.ds(r, 1), :]
    h_hi, h_lo = _split_bf16(h)
    h_ref[...] = h_hi
    p_hi = jnp.dot(h_hi, w2_ref[...], preferred_element_type=F32)
    p_lo = jnp.dot(h_lo, w2_ref[...], preferred_element_type=F32)
    lg_ref[...] = p_hi[:, :ep] + (p_hi[:, ep:] + (p_lo[:, :ep] + p_lo[:, ep:]))


def _norm2(x1, mod, g, w_router_pad, layer, n_p_rows, dec_seq):
    r, d = x1.shape
    tr = 2 * NORM_ROWS
    assert dec_seq % tr == 0 and n_p_rows % tr == 0
    mrows = mod.shape[0]
    ep = w_router_pad.shape[-1]
    return pl.pallas_call(
        functools.partial(_norm2_kernel, n_p=n_p_rows // tr, dec_seq=dec_seq),
        grid=(r // tr,),
        in_specs=[pl.BlockSpec((tr, d), lambda i: (i, 0)),
                  pl.BlockSpec((mrows, d), lambda i: (0, 1)),
                  pl.BlockSpec((mrows, d), lambda i: (0, 2)),
                  pl.BlockSpec((None, 1, d), lambda i: (layer, 0, 0)),
                  pl.BlockSpec((None, d, ep), lambda i: (layer, 0, 0))],
        out_specs=[pl.BlockSpec((tr, d), lambda i: (i, 0)),
                   pl.BlockSpec((tr, ep), lambda i: (i, 0))],
        out_shape=[jax.ShapeDtypeStruct((r, d), BF16), jax.ShapeDtypeStruct((r, ep), F32)],
        scratch_shapes=[pltpu.VMEM((d, 2 * ep), BF16)],
        compiler_params=_params("arbitrary"),
        name="norm2_router",
    )(x1, mod, mod, g.reshape(g.shape[0], 1, d), w_router_pad)


def _route_kernel(lg_ref, h_ref, *refs, n, cap, n_exp, n_req, aliased):
    xs_ref, st_ref, g_ref, s_scr, s32_scr, afft_scr = refs[2 if aliased else 0:]
    ep = lg_ref.shape[1]
    routed = pl.program_id(0) < n_req

    @pl.when(jnp.logical_not(routed))
    def _():
        xs_ref[...] = jnp.zeros(xs_ref.shape, xs_ref.dtype)
        g_ref[...] = jnp.zeros(g_ref.shape, g_ref.dtype)

    @pl.when(routed & (pl.program_id(1) == 0))
    def _():
        lane = lax.broadcasted_iota(I32, (n, ep), 1)
        lg = jnp.where(lane < n_exp, lg_ref[...], -jnp.inf)
        ex = jnp.exp(lg - jnp.max(lg, axis=-1, keepdims=True))
        aff = ex / jnp.sum(ex, axis=-1, keepdims=True)
        afft_scr[...] = aff.T
        slot_id = lax.broadcasted_iota(I32, (cap, n), 0).astype(F32)

        def per_expert(e, carry):
            row = afft_scr[pl.ds(e, 1), :]
            col = jnp.sum(jnp.where(lane == e, aff, 0.0), axis=1, keepdims=True)
            earlier = lax.broadcasted_iota(I32, (n, n), 0) < lax.broadcasted_iota(I32, (n, n), 1)
            beats = jnp.where(earlier, jnp.where(col >= row, 1.0, 0.0), jnp.where(col > row, 1.0, 0.0))
            rank = jnp.sum(beats, axis=0, keepdims=True)
            s_e = (rank == slot_id).astype(F32)
            s32_scr[pl.ds(pl.multiple_of(e * cap, cap), cap), :] = s_e
            g_ref[e] = jnp.sum(s_e * row, axis=1, keepdims=True)
            return carry

        lax.fori_loop(0, n_exp, per_expert, 0, unroll=max(1, 2048 // n))
        s32 = s32_scr[...]
        s_scr[...] = s32.astype(BF16)
        st_ref[...] = s32.T.astype(st_ref.dtype)

    @pl.when(routed)
    def _():
        x = jnp.dot(s_scr[...], h_ref[...], preferred_element_type=F32)
        xs_ref[...] = x.astype(xs_ref.dtype).reshape(xs_ref.shape)


def _route(logits, h2, row0, n_req, n, n_exp, total_slots, prev):
    d = h2.shape[1]
    cap = CAPACITY_FACTOR * n // n_exp
    ecap = n_exp * cap
    ep = logits.shape[1]
    td = d if n * d <= ROUTE_TILE_ELEMS else ROUTE_TILE_ELEMS // n
    rb0 = row0 // n
    sb0 = 0 if prev is None else prev[2]
    n_clear = (total_slots - n_req * cap) // cap if prev is None else 0

    def req(b):
        return jnp.minimum(b, n_req - 1)

    ins = [logits, h2]
    in_specs = [pl.BlockSpec((n, ep), lambda b, t: (rb0 + req(b), 0)),
                pl.BlockSpec((n, td), lambda b, t: (rb0 + req(b), t))]
    aliases = {}
    if prev is not None:
        ins += [prev[0], prev[1]]
        in_specs += [pl.BlockSpec(memory_space=pl.ANY)] * 2
        aliases = {2: 0, 3: 2}
    xs, st, g = pl.pallas_call(
        functools.partial(_route_kernel, n=n, cap=cap, n_exp=n_exp, n_req=n_req, aliased=prev is not None),
        grid=(n_req + n_clear, d // td),
        in_specs=in_specs,
        out_specs=[pl.BlockSpec((n_exp, cap, td), lambda b, t: (0, sb0 + b, t)),
                   pl.BlockSpec((n, ecap), lambda b, t: (req(b), 0)),
                   pl.BlockSpec((n_exp, cap, 1), lambda b, t: (0, sb0 + b, 0))],
        out_shape=[jax.ShapeDtypeStruct((n_exp, total_slots, d), BF16),
                   jax.ShapeDtypeStruct((n_req * n, ecap), BF16),
                   jax.ShapeDtypeStruct((n_exp, total_slots, 1), F32)],
        scratch_shapes=[pltpu.VMEM((ecap, n), BF16), pltpu.VMEM((ecap, n), F32), pltpu.VMEM((ep, n), F32)],
        input_output_aliases=aliases,
        compiler_params=_params("arbitrary", "arbitrary"),
        name="route_gather_n%d" % n,
    )(*ins)
    return xs, st, g


def _ffn_up_kernel(x_ref, wg_ref, wu_ref, o_ref):
    x = x_ref[...]
    a = jnp.dot(x, wg_ref[...].astype(BF16), preferred_element_type=F32)
    u = jnp.dot(x, wu_ref[...].astype(BF16), preferred_element_type=F32)
    o_ref[...] = (jax.nn.silu(a) * u).astype(o_ref.dtype)


def _ffn_up(xs, w_gate, w_up, layer):
    n_exp, slots, d = xs.shape
    f = w_gate.shape[-1]
    tf = FFN_COL_TILE
    wspec = pl.BlockSpec((None, None, d, tf), lambda e, j: (layer, e, 0, j))
    return pl.pallas_call(
        _ffn_up_kernel,
        grid=(n_exp, f // tf),
        in_specs=[pl.BlockSpec((None, slots, d), lambda e, j: (e, 0, 0)), wspec, wspec],
        out_specs=pl.BlockSpec((None, slots, tf), lambda e, j: (e, 0, j)),
        out_shape=jax.ShapeDtypeStruct((n_exp, slots, f), BF16),
        compiler_params=_params("arbitrary", "arbitrary"),
        name="ffn_up",
    )(xs, w_gate, w_up)


def _ffn_down_kernel(h_ref, w_ref, g_ref, o_ref):
    o = jnp.dot(h_ref[...], w_ref[...].astype(BF16), preferred_element_type=F32)
    o_ref[...] = (o * g_ref[...]).astype(o_ref.dtype)


def _ffn_down(hmid, w_down, g, layer):
    n_exp, slots, f = hmid.shape
    d = w_down.shape[-1]
    tn = 2 * COL_TILE
    return pl.pallas_call(
        _ffn_down_kernel,
        grid=(n_exp, d // tn),
        in_specs=[pl.BlockSpec((None, slots, f), lambda e, j: (e, 0, 0)),
                  pl.BlockSpec((None, None, f, tn), lambda e, j: (layer, e, 0, j)),
                  pl.BlockSpec((None, slots, 1), lambda e, j: (e, 0, 0))],
        out_specs=pl.BlockSpec((None, slots, tn), lambda e, j: (e, 0, j)),
        out_shape=jax.ShapeDtypeStruct((n_exp, slots, d), BF16),
        compiler_params=_params("arbitrary", "arbitrary"),
        name="ffn_down",
    )(hmid, w_down, g)


def _combine_kernel(x_ref, st_ref, o_in_ref, gate_ref, fg_ref, o_ref, *, n_k, per_req_mod, final):
    kk = pl.program_id(2)
    mod_row = 1 + pl.program_id(0) if per_req_mod else 0
    part = jnp.dot(st_ref[...], o_in_ref[...].reshape(st_ref.shape[1], o_ref.shape[1]),
                   preferred_element_type=F32)

    @pl.when(kk == 0)
    def _():
        o_ref[...] = part

    @pl.when(kk > 0)
    def _():
        o_ref[...] += part

    @pl.when(kk == n_k - 1)
    def _():
        x2 = x_ref[...] + gate_ref[pl.ds(mod_row, 1), :] * o_ref[...]
        o_ref[...] = _rmsnorm_f32(x2, fg_ref[...]) if final else x2


def _combine(x1, st, o_exp, mod, final_g, row0, n_req, n, slot_blk0, per_req_mod, final):
    d = x1.shape[1]
    n_exp = o_exp.shape[0]
    ecap = st.shape[1]
    cap = ecap // n_exp
    tr = min(n, NORM_ROWS)
    e_chunk = max(1, min(n_exp, COL_TILE // cap))
    n_k = n_exp // e_chunk
    rb0 = row0 // tr
    rt = n // tr
    mrows = mod.shape[0]
    return pl.pallas_call(
        functools.partial(_combine_kernel, n_k=n_k, per_req_mod=per_req_mod, final=final),
        grid=(n_req, rt, n_k),
        in_specs=[pl.BlockSpec((tr, d), lambda b, t, k: (rb0 + b * rt + t, 0)),
                  pl.BlockSpec((tr, e_chunk * cap), lambda b, t, k: (b * rt + t, k)),
                  pl.BlockSpec((e_chunk, cap, d), lambda b, t, k: (k, slot_blk0 + b, 0)),
                  pl.BlockSpec((mrows, d), lambda b, t, k: (0, 3)),
                  pl.BlockSpec((1, d), lambda b, t, k: (0, 0))],
        out_specs=pl.BlockSpec((tr, d), lambda b, t, k: (b * rt + t, 0)),
        out_shape=jax.ShapeDtypeStruct((n_req * n, d), F32),
        compiler_params=_params("arbitrary", "arbitrary", "arbitrary"),
        name="combine_n%d" % n,
    )(x1, st, o_exp, mod, final_g.reshape(1, d))


def kernel(x_prompt, x_sample, cache_k, cache_v, c, c_ctx, w_ada, b_ada, norm1_g, norm2_g, w_in, conv_w, rpb, w_br_att, w_br_conv, w_out, w_router, w_gate, w_up, w_down, final_g):
    n_req, seq, d = x_prompt.shape
    n_dec, dec_seq, _ = x_sample.shape
    depth = w_in.shape[0]
    n_heads, hd = cache_k.shape[-2:]
    a_dim = n_heads * hd
    c_dim = conv_w.shape[-1]
    n_exp = w_router.shape[-1]
    rp, rs = n_req * seq, n_dec * dec_seq
    tm = ROW_TILE
    assert rp % tm == 0 and rs % tm == 0 and dec_seq % NORM_ROWS == 0 and seq % NORM_ROWS == 0

    xp = x_prompt.reshape(rp, d)
    xs = x_sample.reshape(rs, d)
    ck = cache_k.reshape(cache_k.shape[:3] + (a_dim,))
    cv = cache_v.reshape(cache_v.shape[:3] + (a_dim,))
    cond = jnp.concatenate([c_ctx[None, :], c], axis=0)
    cond = jnp.pad(cond, ((0, -cond.shape[0] % SUBLANES), (0, 0)))
    w_router_pad = jnp.pad(w_router, ((0, 0), (0, 0), (0, -n_exp % LANES)))
    cap_p = CAPACITY_FACTOR * seq // n_exp
    cap_s = CAPACITY_FACTOR * dec_seq // n_exp
    slots = n_req * cap_p + n_dec * cap_s

    new_k, new_v = [], []
    for l in range(depth):
        last = l == depth - 1
        mod1 = _adaln(cond, w_ada, b_ada, l, 2 * d)
        h = _norm1(xp, xs, mod1, norm1_g, l, dec_seq)
        q = _proj(h, w_in, l, 0, a_dim, BF16, "proj_q")
        k_p, k_s = _proj_split(h, w_in, l, a_dim, a_dim, rp, "proj_k")
        v_p, v_s = _proj_split(h, w_in, l, 2 * a_dim, a_dim, rp, "proj_v")
        y_conv = _conv_proj(h, w_in, conv_w, l, 3 * a_dim, c_dim, rp, seq, dec_seq)
        gates, mod = _proj_ada(h, w_in, l, 3 * a_dim + 3 * c_dim, 2 * d, cond.T, 1 + n_dec,
                               w_ada, b_ada, 2 * d, "proj_gates")
        new_k.append(k_p.reshape(n_req, seq, n_heads, hd))
        new_v.append(v_p.reshape(n_req, seq, n_heads, hd))

        y_att_p = _ctx_attention(q, k_p, v_p, n_req, seq, n_heads, hd)
        y_att_s = _nbr_attention(rpb, q, k_s, v_s, ck, cv, l, n_heads, hd)
        merged = _merge(y_att_p, y_att_s, y_conv, gates, w_br_att, w_br_conv, l, 0, d)
        x1 = _outproj(merged, w_out, xp, xs, mod, l, dec_seq)

        h2, logits = _norm2(x1, mod, norm2_g, w_router_pad, l, rp, dec_seq)
        xg, st_p, g = _route(logits, h2, 0, n_req, seq, n_exp, slots, None)
        xg, st_s, g = _route(logits, h2, rp, n_dec, dec_seq, n_exp, slots,
                             (xg, g, n_req * cap_p // cap_s))
        o_exp = _ffn_down(_ffn_up(xg, w_gate, w_up, l), w_down, g, l)
        xp = _combine(x1, st_p, o_exp, mod, final_g, 0, n_req, seq, 0, False, last)
        xs = _combine(x1, st_s, o_exp, mod, final_g, rp, n_dec, dec_seq,
                      n_req * cap_p // cap_s, True, last)

    y_prompt = xp.reshape(n_req, seq, d)
    y_sample = xs.reshape(n_dec, dec_seq, d)
    return (y_prompt, y_sample, jnp.stack(new_k, axis=1), jnp.stack(new_v, axis=1))
```

```python
import functools

import jax
import jax.numpy as jnp
from jax import lax
from jax.experimental import pallas as pl
from jax.experimental.pallas import tpu as pltpu

BF16 = jnp.bfloat16
F32 = jnp.float32
I32 = jnp.int32

GRID_W = 64
WIN_ROWS = 8
WIN_COLS = 16
CAPACITY_FACTOR = 2
N_MOD = 6
RMS_EPS = 1e-6

LANES = 128
SUBLANES = 8
VMEM_LIMIT_BYTES = 56 * 1024 * 1024

ROW_TILE = 1024
COL_TILE = 512
CONV_COL_TILE = 256
FFN_COL_TILE = 256
ROUTE_TILE_ELEMS = 1024 * 1024
NORM_ROWS = 256
ROW_CHUNK = 16
CHUNK_UNROLL = 4

_NT = (((1,), (1,)), ((), ()))


def _params(*sem):
    return pltpu.CompilerParams(dimension_semantics=sem, vmem_limit_bytes=VMEM_LIMIT_BYTES)


def _mod_row(tile, n_prompt_tiles, tile_rows, dec_seq):
    req = lax.div(jnp.maximum(tile - n_prompt_tiles, 0) * tile_rows, dec_seq)
    return jnp.where(tile < n_prompt_tiles, 0, 1 + req)


def _rmsnorm_f32(x, g):
    return x * lax.rsqrt(jnp.mean(x * x, axis=-1, keepdims=True) + RMS_EPS) * g


def _for_row_chunks(n_rows, fn):
    def body(c, carry):
        fn(pl.ds(pl.multiple_of(c * ROW_CHUNK, ROW_CHUNK), ROW_CHUNK))
        return carry

    lax.fori_loop(0, n_rows // ROW_CHUNK, body, 0, unroll=CHUNK_UNROLL)


def _adaln_kernel(c_ref, w_ref, b_ref, o_ref):
    s = jax.nn.silu(c_ref[...])
    o_ref[...] = jnp.dot(s.astype(BF16), w_ref[...].astype(BF16), preferred_element_type=F32) + b_ref[...]


def _adaln(cond, w_ada, b_ada, layer, n):
    rows, d = cond.shape
    tn = COL_TILE
    return pl.pallas_call(
        _adaln_kernel,
        grid=(n // tn,),
        in_specs=[pl.BlockSpec((rows, d), lambda j: (0, 0)),
                  pl.BlockSpec((None, d, tn), lambda j: (layer, 0, j)),
                  pl.BlockSpec((None, 1, tn), lambda j: (layer, 0, j))],
        out_specs=pl.BlockSpec((rows, tn), lambda j: (0, j)),
        out_shape=jax.ShapeDtypeStruct((rows, n), F32),
        compiler_params=_params("arbitrary"),
        name="adaln",
    )(cond, w_ada, b_ada.reshape(b_ada.shape[0], 1, -1))


def _norm1_kernel(xp_ref, xs_ref, sh_ref, sc_ref, g_ref, o_ref, *, n_p, dec_seq):
    i = pl.program_id(0)
    r = _mod_row(i, n_p, o_ref.shape[0], dec_seq)
    shift = sh_ref[pl.ds(r, 1), :]
    scale = sc_ref[pl.ds(r, 1), :]

    def emit(x_ref):
        def chunk(rs):
            y = _rmsnorm_f32(x_ref[rs, :], g_ref[...])
            o_ref[rs, :] = (y * (1.0 + scale) + shift).astype(o_ref.dtype)

        _for_row_chunks(o_ref.shape[0], chunk)

    pl.when(i < n_p)(lambda: emit(xp_ref))
    pl.when(i >= n_p)(lambda: emit(xs_ref))


def _norm1(xp, xs, mod, g, layer, dec_seq):
    rp, d = xp.shape
    rs = xs.shape[0]
    tr = 2 * NORM_ROWS
    assert dec_seq % tr == 0 and rp % tr == 0
    n_p, n_s = rp // tr, rs // tr
    mrows = mod.shape[0]
    return pl.pallas_call(
        functools.partial(_norm1_kernel, n_p=n_p, dec_seq=dec_seq),
        grid=(n_p + n_s,),
        in_specs=[pl.BlockSpec((tr, d), lambda i: (jnp.minimum(i, n_p - 1), 0)),
                  pl.BlockSpec((tr, d), lambda i: (jnp.maximum(i - n_p, 0), 0)),
                  pl.BlockSpec((mrows, d), lambda i: (0, 0)),
                  pl.BlockSpec((mrows, d), lambda i: (0, 1)),
                  pl.BlockSpec((None, 1, d), lambda i: (layer, 0, 0))],
        out_specs=pl.BlockSpec((tr, d), lambda i: (i, 0)),
        out_shape=jax.ShapeDtypeStruct((rp + rs, d), BF16),
        compiler_params=_params("arbitrary"),
        name="norm1",
    )(xp, xs, mod, mod, g.reshape(g.shape[0], 1, d))


def _proj_kernel(a_ref, b_ref, o_ref, bs_ref):
    @pl.when(pl.program_id(1) == 0)
    def _():
        bs_ref[...] = b_ref[...].astype(BF16)

    o_ref[...] = jnp.dot(a_ref[...], bs_ref[...], preferred_element_type=F32).astype(o_ref.dtype)


def _proj(a, w, layer, col0, ncols, out_dtype, name):
    m, k = a.shape
    tm, tn = ROW_TILE, COL_TILE
    cb0 = col0 // tn
    return pl.pallas_call(
        _proj_kernel,
        grid=(ncols // tn, m // tm),
        in_specs=[pl.BlockSpec((tm, k), lambda j, i: (i, 0)),
                  pl.BlockSpec((None, k, tn), lambda j, i: (layer, 0, cb0 + j))],
        out_specs=pl.BlockSpec((tm, tn), lambda j, i: (i, j)),
        out_shape=jax.ShapeDtypeStruct((m, ncols), out_dtype),
        scratch_shapes=[pltpu.VMEM((k, tn), BF16)],
        compiler_params=_params("arbitrary", "arbitrary"),
        name=name,
    )(a, w)


def _proj_ada_kernel(a_ref, b_ref, ct_ref, wa_ref, ba_ref, o_ref, mod_ref, bs_ref, sb_ref, *, n_cond):
    j, i = pl.program_id(0), pl.program_id(1)

    @pl.when(i == 0)
    def _():
        bs_ref[...] = b_ref[...].astype(BF16)

    @pl.when((i == 0) & (j == 0))
    def _():
        s_t = jax.nn.silu(ct_ref[...])
        for r in range(n_cond):
            sb_ref[r] = jnp.broadcast_to(s_t[:, r:r + 1], sb_ref.shape[1:])

    o_ref[...] = jnp.dot(a_ref[...], bs_ref[...], preferred_element_type=F32).astype(o_ref.dtype)
    w = wa_ref[...]
    sub = lax.broadcasted_iota(I32, mod_ref.shape, 0)
    acc = jnp.zeros(mod_ref.shape, F32)
    for r in range(n_cond):
        row = jnp.sum(sb_ref[r] * w, axis=0, keepdims=True)
        acc = jnp.where(sub == r, row, acc)
    mod_ref[...] = acc + ba_ref[...]


def _proj_ada(a, w, layer, col0, ncols, cond_t, n_cond, w_ada, b_ada, ada_col0, name):
    m, k = a.shape
    d, crow = cond_t.shape
    tm, tn = ROW_TILE, COL_TILE
    cb0 = col0 // tn
    n_i = m // tm
    n_ada = w_ada.shape[-1] - ada_col0
    n_blocks = n_ada // LANES
    assert n_blocks <= (ncols // tn) * n_i and ada_col0 % LANES == 0
    ab0 = ada_col0 // LANES

    def ada_blk(j, i):
        return ab0 + jnp.minimum(j * n_i + i, n_blocks - 1)

    return pl.pallas_call(
        functools.partial(_proj_ada_kernel, n_cond=n_cond),
        grid=(ncols // tn, n_i),
        in_specs=[pl.BlockSpec((tm, k), lambda j, i: (i, 0)),
                  pl.BlockSpec((None, k, tn), lambda j, i: (layer, 0, cb0 + j)),
                  pl.BlockSpec((d, crow), lambda j, i: (0, 0)),
                  pl.BlockSpec((None, d, LANES), lambda j, i: (layer, 0, ada_blk(j, i))),
                  pl.BlockSpec((None, 1, LANES), lambda j, i: (layer, 0, ada_blk(j, i)))],
        out_specs=[pl.BlockSpec((tm, tn), lambda j, i: (i, j)),
                   pl.BlockSpec((crow, LANES), lambda j, i: (0, ada_blk(j, i) - ab0))],
        out_shape=[jax.ShapeDtypeStruct((m, ncols), BF16),
                   jax.ShapeDtypeStruct((crow, n_ada), F32)],
        scratch_shapes=[pltpu.VMEM((k, tn), BF16), pltpu.VMEM((n_cond, d, LANES), F32)],
        compiler_params=_params("arbitrary", "arbitrary"),
        name=name,
    )(a, w, cond_t, w_ada, b_ada.reshape(b_ada.shape[0], 1, -1))


def _proj_split_kernel(a_ref, b_ref, op_ref, os_ref, bs_ref, *, n_p):
    i = pl.program_id(1)

    @pl.when(i == 0)
    def _():
        bs_ref[...] = b_ref[...].astype(BF16)

    res = jnp.dot(a_ref[...], bs_ref[...], preferred_element_type=F32)

    @pl.when(i < n_p)
    def _():
        op_ref[...] = res

    @pl.when(i >= n_p)
    def _():
        os_ref[...] = res


def _proj_split(a, w, layer, col0, ncols, n_prompt_rows, name):
    m, k = a.shape
    tm, tn = ROW_TILE, COL_TILE
    cb0 = col0 // tn
    n_p = n_prompt_rows // tm
    return pl.pallas_call(
        functools.partial(_proj_split_kernel, n_p=n_p),
        grid=(ncols // tn, m // tm),
        in_specs=[pl.BlockSpec((tm, k), lambda j, i: (i, 0)),
                  pl.BlockSpec((None, k, tn), lambda j, i: (layer, 0, cb0 + j))],
        out_specs=[pl.BlockSpec((tm, tn), lambda j, i: (jnp.minimum(i, n_p - 1), j)),
                   pl.BlockSpec((tm, tn), lambda j, i: (jnp.maximum(i - n_p, 0), j))],
        out_shape=[jax.ShapeDtypeStruct((n_prompt_rows, ncols), F32),
                   jax.ShapeDtypeStruct((m - n_prompt_rows, ncols), F32)],
        scratch_shapes=[pltpu.VMEM((k, tn), BF16)],
        compiler_params=_params("arbitrary", "arbitrary"),
        name=name,
    )(a, w)


def _seq_pos(rows, seq):
    t = lax.broadcasted_iota(I32, (rows, 1), 0)
    return t & (seq - 1) if seq & (seq - 1) == 0 else lax.rem(t, seq)


def _conv_proj_kernel(a_ref, wu_ref, wb_ref, wc_ref, cw_ref, o_ref, us_ref, bs_ref, cs_ref, *, n_p, seq_p, seq_s):
    i = pl.program_id(1)
    tm = o_ref.shape[0]

    @pl.when(i == 0)
    def _():
        us_ref[...] = wu_ref[...].astype(BF16)
        bs_ref[...] = wb_ref[...].astype(BF16)
        cs_ref[...] = wc_ref[...].astype(BF16)

    a = a_ref[...]
    dot = functools.partial(jnp.dot, preferred_element_type=F32)
    z = dot(a, cs_ref[...]) * dot(a, us_ref[...])
    prompt = i < n_p
    pos = jnp.where(prompt, _seq_pos(tm, seq_p), _seq_pos(tm, seq_s))
    last = jnp.where(prompt, seq_p - 1, seq_s - 1)
    z_prev = jnp.where(pos == 0, 0.0, pltpu.roll(z, 1, 0))
    z_next = jnp.where(pos == last, 0.0, pltpu.roll(z, tm - 1, 0))
    w = cw_ref[...]
    y = z_prev * w[0:1, :] + z * w[1:2, :] + z_next * w[2:3, :]
    o_ref[...] = (dot(a, bs_ref[...]) * y).astype(o_ref.dtype)


def _conv_proj(h, w_in, conv_w, layer, col0, c_dim, n_p_rows, seq_p, seq_s):
    r, k = h.shape
    tm, tn = ROW_TILE, CONV_COL_TILE
    assert tm % seq_p == 0 and tm % seq_s == 0, "row tile must hold whole sequences"
    assert conv_w.shape[1] == 3, "centred width-3 convolution"
    cb0, ncb = col0 // tn, c_dim // tn

    def wspec(part):
        return pl.BlockSpec((None, k, tn), lambda j, i: (layer, 0, cb0 + part * ncb + j))

    return pl.pallas_call(
        functools.partial(_conv_proj_kernel, n_p=n_p_rows // tm, seq_p=seq_p, seq_s=seq_s),
        grid=(ncb, r // tm),
        in_specs=[pl.BlockSpec((tm, k), lambda j, i: (i, 0)), wspec(0), wspec(1), wspec(2),
                  pl.BlockSpec((None, 3, tn), lambda j, i: (layer, 0, j))],
        out_specs=pl.BlockSpec((tm, tn), lambda j, i: (i, j)),
        out_shape=jax.ShapeDtypeStruct((r, c_dim), BF16),
        scratch_shapes=[pltpu.VMEM((k, tn), BF16)] * 3,
        compiler_params=_params("arbitrary", "arbitrary"),
        name="proj_conv",
    )(h, w_in, w_in, w_in, conv_w)


def _softmax_rows(parts):
    m = functools.reduce(jnp.maximum, [jnp.max(s, axis=-1, keepdims=True) for s in parts])
    es = [jnp.exp(s - m) for s in parts]
    inv = 1.0 / functools.reduce(jnp.add, [jnp.sum(e, axis=-1, keepdims=True) for e in es])
    return [e * inv for e in es]


def _ctx_attn_kernel(q_ref, k_ref, v_ref, o_ref, *, n_heads, hd):
    scale = hd ** -0.5
    for h in range(n_heads):
        sl = slice(h * hd, (h + 1) * hd)
        s = lax.dot_general(q_ref[:, sl], k_ref[:, sl].astype(BF16), _NT, preferred_element_type=F32) * scale
        e = jnp.exp(s - jnp.max(s, axis=-1, keepdims=True)).astype(BF16)
        v = v_ref[:, sl].astype(BF16)
        oa = jnp.dot(e, jnp.concatenate([v, jnp.ones_like(v)], axis=1), preferred_element_type=F32)
        o_ref[:, sl] = (oa[:, :hd] / oa[:, hd:]).astype(o_ref.dtype)


def _ctx_attention(q, k_p, v_p, n_req, seq, n_heads, hd):
    a = q.shape[1]
    return pl.pallas_call(
        functools.partial(_ctx_attn_kernel, n_heads=n_heads, hd=hd),
        grid=(n_req,),
        in_specs=[pl.BlockSpec((seq, a), lambda b: (b, 0))] * 3,
        out_specs=pl.BlockSpec((seq, a), lambda b: (b, 0)),
        out_shape=jax.ShapeDtypeStruct((n_req * seq, a), BF16),
        compiler_params=_params("arbitrary"),
        name="ctx_attention",
    )(q, k_p, v_p)


def _na_kernel(rpb_ref, q_ref, k_ref, v_ref, ck_ref, cv_ref, o_ref, pairs_ref, *, layer, n_heads, hd, rows, kr):
    w, wr, wc = GRID_W, WIN_ROWS, WIN_COLS
    ndr, ndc = 2 * wr - 1, 2 * wc - 1
    scale = hd ** -0.5

    @pl.when(pl.program_id(1) == 0)
    def _():
        base = (layer * n_heads + pl.program_id(0)) * (ndr * ndc)
        qc = lax.broadcasted_iota(I32, (w, 2 * w), 0)
        lane = lax.broadcasted_iota(I32, (w, 2 * w), 1)
        second = lane >= w
        kc = jnp.where(second, lane - w, lane)
        c0 = jnp.clip(qc - wc // 2, 0, w - wc)
        in_win = (kc >= c0) & (kc < c0 + wc)
        dc_idx = jnp.clip(kc - qc + wc - 1, 0, ndc - 1)
        dc_masks = [dc_idx == dc for dc in range(ndc)]
        for dr in range(ndr - 1):
            t = jnp.zeros((w, 2 * w), F32)
            for dc in range(ndc):
                va = rpb_ref[base + dr * ndc + dc]
                vb = rpb_ref[base + (dr + 1) * ndc + dc]
                t = jnp.where(dc_masks[dc], jnp.where(second, vb, va), t)
            pairs_ref[dr] = jnp.where(in_win, t, -jnp.inf)

    q = q_ref[...]
    kb = k_ref[...].astype(BF16)
    vb16 = v_ref[...].astype(BF16)
    cvb = cv_ref[...].astype(BF16)
    s_ctx = lax.dot_general(q, ck_ref[...].astype(BF16), _NT, preferred_element_type=F32) * scale

    groups = []
    for r in range(rows):
        r0 = min(max(r - kr // 2, 0), rows - kr)
        if groups and groups[-1][2] == r0:
            groups[-1][1] += 1
        else:
            groups.append([r, 1, r0])

    s_loc = []
    for r_first, cnt, r0 in groups:
        qs = slice(r_first * w, (r_first + cnt) * w)
        ks = slice(r0 * w, (r0 + kr) * w)
        bias = jnp.concatenate(
            [jnp.concatenate([pairs_ref[r0 - r + wr - 1 + 2 * m] for m in range(kr // 2)], axis=1)
             for r in range(r_first, r_first + cnt)], axis=0)
        s_loc.append(lax.dot_general(q[qs], kb[ks], _NT, preferred_element_type=F32) * scale + bias)
    p_loc, p_ctx = _softmax_rows([jnp.concatenate(s_loc, axis=0), s_ctx])
    p_loc = p_loc.astype(BF16)
    o_ctx = jnp.dot(p_ctx.astype(BF16), cvb, preferred_element_type=F32)
    for r_first, cnt, r0 in groups:
        qs = slice(r_first * w, (r_first + cnt) * w)
        ks = slice(r0 * w, (r0 + kr) * w)
        o = jnp.dot(p_loc[qs], vb16[ks], preferred_element_type=F32) + o_ctx[qs]
        o_ref[qs, :] = o.astype(o_ref.dtype)


def _nbr_attention(rpb, q, k_s, v_s, cache_k, cache_v, layer, n_heads, hd):
    n_dec, _, past, _ = cache_k.shape
    t = k_s.shape[0] // n_dec
    rows = t // GRID_W
    kr = min(WIN_ROWS, rows)
    assert kr % 2 == 0 and t % GRID_W == 0
    blk0 = (q.shape[0] - k_s.shape[0]) // t
    tok = pl.BlockSpec((t, hd), lambda h, b: (b, h))
    tok_q = pl.BlockSpec((t, hd), lambda h, b: (blk0 + b, h))
    cache = pl.BlockSpec((None, None, past, hd), lambda h, b: (b, layer, 0, h))
    return pl.pallas_call(
        functools.partial(_na_kernel, layer=layer, n_heads=n_heads, hd=hd, rows=rows, kr=kr),
        grid=(n_heads, n_dec),
        in_specs=[pl.BlockSpec(memory_space=pltpu.SMEM), tok_q, tok, tok, cache, cache],
        out_specs=tok,
        out_shape=jax.ShapeDtypeStruct(k_s.shape, BF16),
        scratch_shapes=[pltpu.VMEM((2 * WIN_ROWS - 2, GRID_W, 2 * GRID_W), F32)],
        compiler_params=_params("arbitrary", "arbitrary"),
        name="nbr_attention",
    )(rpb.reshape(-1), q, k_s, v_s, cache_k, cache_v)


def _merge_kernel(yap_ref, yas_ref, yc_ref, ga_ref, gb_ref, wa_ref, wc_ref, o_ref, was_ref, wcs_ref, *, n_p):
    i = pl.program_id(1)

    @pl.when(i == 0)
    def _():
        was_ref[...] = wa_ref[...].astype(BF16)
        wcs_ref[...] = wc_ref[...].astype(BF16)

    def emit(ya_ref):
        a = jnp.dot(ya_ref[...], was_ref[...], preferred_element_type=F32)
        c = jnp.dot(yc_ref[...], wcs_ref[...], preferred_element_type=F32)
        o = jax.nn.sigmoid(ga_ref[...].astype(F32)) * a + jax.nn.sigmoid(gb_ref[...].astype(F32)) * c
        o_ref[...] = o.astype(o_ref.dtype)

    pl.when(i < n_p)(lambda: emit(yap_ref))
    pl.when(i >= n_p)(lambda: emit(yas_ref))


def _merge(y_att_p, y_att_s, y_conv, rest, w_br_att, w_br_conv, layer, gate_col0, d):
    a_dim = y_att_p.shape[1]
    r, c_dim = y_conv.shape
    tm, tn = ROW_TILE, COL_TILE
    gcb = gate_col0 // tn
    ndb = d // tn
    n_p = y_att_p.shape[0] // tm
    return pl.pallas_call(
        functools.partial(_merge_kernel, n_p=n_p),
        grid=(ndb, r // tm),
        in_specs=[pl.BlockSpec((tm, a_dim), lambda j, i: (jnp.minimum(i, n_p - 1), 0)),
                  pl.BlockSpec((tm, a_dim), lambda j, i: (jnp.maximum(i - n_p, 0), 0)),
                  pl.BlockSpec((tm, c_dim), lambda j, i: (i, 0)),
                  pl.BlockSpec((tm, tn), lambda j, i: (i, gcb + j)),
                  pl.BlockSpec((tm, tn), lambda j, i: (i, gcb + ndb + j)),
                  pl.BlockSpec((None, a_dim, tn), lambda j, i: (layer, 0, j)),
                  pl.BlockSpec((None, c_dim, tn), lambda j, i: (layer, 0, j))],
        out_specs=pl.BlockSpec((tm, tn), lambda j, i: (i, j)),
        out_shape=jax.ShapeDtypeStruct((r, d), BF16),
        scratch_shapes=[pltpu.VMEM((a_dim, tn), BF16), pltpu.VMEM((c_dim, tn), BF16)],
        compiler_params=_params("arbitrary", "arbitrary"),
        name="merge",
    )(y_att_p, y_att_s, y_conv, rest, rest, w_br_att, w_br_conv)


def _outproj_kernel(m_ref, w_ref, xp_ref, xs_ref, gate_ref, o_ref, ws_ref, *, n_p, dec_seq):
    i = pl.program_id(1)

    @pl.when(i == 0)
    def _():
        ws_ref[...] = w_ref[...].astype(BF16)

    gate = gate_ref[pl.ds(_mod_row(i, n_p, o_ref.shape[0], dec_seq), 1), :]
    upd = gate * jnp.dot(m_ref[...], ws_ref[...], preferred_element_type=F32)

    @pl.when(i < n_p)
    def _():
        o_ref[...] = xp_ref[...] + upd

    @pl.when(i >= n_p)
    def _():
        o_ref[...] = xs_ref[...] + upd


def _outproj(merged, w_out, xp, xs, mod, layer, dec_seq):
    r, d = merged.shape
    tm, tn = ROW_TILE, COL_TILE
    n_p = xp.shape[0] // tm
    mrows = mod.shape[0]
    ndb = d // tn
    return pl.pallas_call(
        functools.partial(_outproj_kernel, n_p=n_p, dec_seq=dec_seq),
        grid=(ndb, r // tm),
        in_specs=[pl.BlockSpec((tm, d), lambda j, i: (i, 0)),
                  pl.BlockSpec((None, d, tn), lambda j, i: (layer, 0, j)),
                  pl.BlockSpec((tm, tn), lambda j, i: (jnp.minimum(i, n_p - 1), j)),
                  pl.BlockSpec((tm, tn), lambda j, i: (jnp.maximum(i - n_p, 0), j)),
                  pl.BlockSpec((mrows, tn), lambda j, i: (0, j))],
        out_specs=pl.BlockSpec((tm, tn), lambda j, i: (i, j)),
        out_shape=jax.ShapeDtypeStruct((r, d), F32),
        scratch_shapes=[pltpu.VMEM((d, tn), BF16)],
        compiler_params=_params("arbitrary", "arbitrary"),
        name="outproj",
    )(merged, w_out, xp, xs, mod)


def _split_bf16(x):
    hi = x.astype(BF16)
    return hi, (x - hi.astype(F32)).astype(BF16)


def _norm2_kernel(x_ref, sh_ref, sc_ref, g_ref, wr_ref, h_ref, lg_ref, w2_ref, *, n_p, dec_seq):
    i = pl.program_id(0)
    ep = lg_ref.shape[1]

    @pl.when(i == 0)
    def _():
        w_hi, w_lo = _split_bf16(wr_ref[...])
        w2_ref[:, :ep] = w_hi
        w2_ref[:, ep:] = w_lo

    r = _mod_row(i, n_p, h_ref.shape[0], dec_seq)
    y = _rmsnorm_f32(x_ref[...], g_ref[...])
    h = y * (1.0 + sc_ref[pl.ds(r, 1), :]) + sh_ref[pl.ds(r, 1), :]
    h_hi, h_lo = _split_bf16(h)
    h_ref[...] = h_hi
    p_hi = jnp.dot(h_hi, w2_ref[...], preferred_element_type=F32)
    p_lo = jnp.dot(h_lo, w2_ref[...], preferred_element_type=F32)
    lg_ref[...] = p_hi[:, :ep] + (p_hi[:, ep:] + (p_lo[:, :ep] + p_lo[:, ep:]))


def _norm2(x1, mod, g, w_router_pad, layer, n_p_rows, dec_seq):
    r, d = x1.shape
    tr = 2 * NORM_ROWS
    assert dec_seq % tr == 0 and n_p_rows % tr == 0
    mrows = mod.shape[0]
    ep = w_router_pad.shape[-1]
    return pl.pallas_call(
        functools.partial(_norm2_kernel, n_p=n_p_rows // tr, dec_seq=dec_seq),
        grid=(r // tr,),
        in_specs=[pl.BlockSpec((tr, d), lambda i: (i, 0)),
                  pl.BlockSpec((mrows, d), lambda i: (0, 1)),
                  pl.BlockSpec((mrows, d), lambda i: (0, 2)),
                  pl.BlockSpec((None, 1, d), lambda i: (layer, 0, 0)),
                  pl.BlockSpec((None, d, ep), lambda i: (layer, 0, 0))],
        out_specs=[pl.BlockSpec((tr, d), lambda i: (i, 0)),
                   pl.BlockSpec((tr, ep), lambda i: (i, 0))],
        out_shape=[jax.ShapeDtypeStruct((r, d), BF16), jax.ShapeDtypeStruct((r, ep), F32)],
        scratch_shapes=[pltpu.VMEM((d, 2 * ep), BF16)],
        compiler_params=_params("arbitrary"),
        name="norm2_router",
    )(x1, mod, mod, g.reshape(g.shape[0], 1, d), w_router_pad)


def _route_kernel(lg_ref, h_ref, *refs, n, cap, n_exp, n_req, aliased):
    xs_ref, st_ref, g_ref, s_scr, s32_scr, afft_scr = refs[2 if aliased else 0:]
    ep = lg_ref.shape[1]
    routed = pl.program_id(0) < n_req

    @pl.when(jnp.logical_not(routed))
    def _():
        xs_ref[...] = jnp.zeros(xs_ref.shape, xs_ref.dtype)
        g_ref[...] = jnp.zeros(g_ref.shape, g_ref.dtype)

    @pl.when(routed & (pl.program_id(1) == 0))
    def _():
        lane = lax.broadcasted_iota(I32, (n, ep), 1)
        lg = jnp.where(lane < n_exp, lg_ref[...], -jnp.inf)
        ex = jnp.exp(lg - jnp.max(lg, axis=-1, keepdims=True))
        aff = ex / jnp.sum(ex, axis=-1, keepdims=True)
        afft_scr[...] = aff.T
        slot_id = lax.broadcasted_iota(I32, (cap, n), 0).astype(F32)

        def per_expert(e, carry):
            row = afft_scr[pl.ds(e, 1), :]
            col = jnp.sum(jnp.where(lane == e, aff, 0.0), axis=1, keepdims=True)
            earlier = lax.broadcasted_iota(I32, (n, n), 0) < lax.broadcasted_iota(I32, (n, n), 1)
            beats = jnp.where(earlier, jnp.where(col >= row, 1.0, 0.0), jnp.where(col > row, 1.0, 0.0))
            rank = jnp.sum(beats, axis=0, keepdims=True)
            s_e = (rank == slot_id).astype(F32)
            s32_scr[pl.ds(pl.multiple_of(e * cap, cap), cap), :] = s_e
            g_ref[e] = jnp.sum(s_e * row, axis=1, keepdims=True)
            return carry

        lax.fori_loop(0, n_exp, per_expert, 0, unroll=max(1, 2048 // n))
        s32 = s32_scr[...]
        s_scr[...] = s32.astype(BF16)
        st_ref[...] = s32.T.astype(st_ref.dtype)

    @pl.when(routed)
    def _():
        x = jnp.dot(s_scr[...], h_ref[...], preferred_element_type=F32)
        xs_ref[...] = x.astype(xs_ref.dtype).reshape(xs_ref.shape)


def _route(logits, h2, row0, n_req, n, n_exp, total_slots, prev):
    d = h2.shape[1]
    cap = CAPACITY_FACTOR * n // n_exp
    ecap = n_exp * cap
    ep = logits.shape[1]
    td = d if n * d <= ROUTE_TILE_ELEMS else ROUTE_TILE_ELEMS // n
    rb0 = row0 // n
    sb0 = 0 if prev is None else prev[2]
    n_clear = (total_slots - n_req * cap) // cap if prev is None else 0

    def req(b):
        return jnp.minimum(b, n_req - 1)

    ins = [logits, h2]
    in_specs = [pl.BlockSpec((n, ep), lambda b, t: (rb0 + req(b), 0)),
                pl.BlockSpec((n, td), lambda b, t: (rb0 + req(b), t))]
    aliases = {}
    if prev is not None:
        ins += [prev[0], prev[1]]
        in_specs += [pl.BlockSpec(memory_space=pl.ANY)] * 2
        aliases = {2: 0, 3: 2}
    xs, st, g = pl.pallas_call(
        functools.partial(_route_kernel, n=n, cap=cap, n_exp=n_exp, n_req=n_req, aliased=prev is not None),
        grid=(n_req + n_clear, d // td),
        in_specs=in_specs,
        out_specs=[pl.BlockSpec((n_exp, cap, td), lambda b, t: (0, sb0 + b, t)),
                   pl.BlockSpec((n, ecap), lambda b, t: (req(b), 0)),
                   pl.BlockSpec((n_exp, cap, 1), lambda b, t: (0, sb0 + b, 0))],
        out_shape=[jax.ShapeDtypeStruct((n_exp, total_slots, d), BF16),
                   jax.ShapeDtypeStruct((n_req * n, ecap), BF16),
                   jax.ShapeDtypeStruct((n_exp, total_slots, 1), F32)],
        scratch_shapes=[pltpu.VMEM((ecap, n), BF16), pltpu.VMEM((ecap, n), F32), pltpu.VMEM((ep, n), F32)],
        input_output_aliases=aliases,
        compiler_params=_params("arbitrary", "arbitrary"),
        name="route_gather_n%d" % n,
    )(*ins)
    return xs, st, g


def _ffn_up_kernel(x_ref, wg_ref, wu_ref, o_ref):
    x = x_ref[...]
    a = jnp.dot(x, wg_ref[...].astype(BF16), preferred_element_type=F32)
    u = jnp.dot(x, wu_ref[...].astype(BF16), preferred_element_type=F32)
    o_ref[...] = (jax.nn.silu(a) * u).astype(o_ref.dtype)


def _ffn_up(xs, w_gate, w_up, layer):
    n_exp, slots, d = xs.shape
    f = w_gate.shape[-1]
    tf = FFN_COL_TILE
    wspec = pl.BlockSpec((None, None, d, tf), lambda e, j: (layer, e, 0, j))
    return pl.pallas_call(
        _ffn_up_kernel,
        grid=(n_exp, f // tf),
        in_specs=[pl.BlockSpec((None, slots, d), lambda e, j: (e, 0, 0)), wspec, wspec],
        out_specs=pl.BlockSpec((None, slots, tf), lambda e, j: (e, 0, j)),
        out_shape=jax.ShapeDtypeStruct((n_exp, slots, f), BF16),
        compiler_params=_params("arbitrary", "arbitrary"),
        name="ffn_up",
    )(xs, w_gate, w_up)


def _ffn_down_kernel(h_ref, w_ref, g_ref, o_ref):
    o = jnp.dot(h_ref[...], w_ref[...].astype(BF16), preferred_element_type=F32)
    o_ref[...] = (o * g_ref[...]).astype(o_ref.dtype)


def _ffn_down(hmid, w_down, g, layer):
    n_exp, slots, f = hmid.shape
    d = w_down.shape[-1]
    tn = 2 * COL_TILE
    return pl.pallas_call(
        _ffn_down_kernel,
        grid=(n_exp, d // tn),
        in_specs=[pl.BlockSpec((None, slots, f), lambda e, j: (e, 0, 0)),
                  pl.BlockSpec((None, None, f, tn), lambda e, j: (layer, e, 0, j)),
                  pl.BlockSpec((None, slots, 1), lambda e, j: (e, 0, 0))],
        out_specs=pl.BlockSpec((None, slots, tn), lambda e, j: (e, 0, j)),
        out_shape=jax.ShapeDtypeStruct((n_exp, slots, d), BF16),
        compiler_params=_params("arbitrary", "arbitrary"),
        name="ffn_down",
    )(hmid, w_down, g)


def _combine_kernel(x_ref, st_ref, o_in_ref, gate_ref, fg_ref, o_ref, *, n_k, per_req_mod, final):
    kk = pl.program_id(2)
    mod_row = 1 + pl.program_id(0) if per_req_mod else 0
    part = jnp.dot(st_ref[...], o_in_ref[...].reshape(st_ref.shape[1], o_ref.shape[1]),
                   preferred_element_type=F32)

    @pl.when(kk == 0)
    def _():
        o_ref[...] = part

    @pl.when(kk > 0)
    def _():
        o_ref[...] += part

    @pl.when(kk == n_k - 1)
    def _():
        x2 = x_ref[...] + gate_ref[pl.ds(mod_row, 1), :] * o_ref[...]
        o_ref[...] = _rmsnorm_f32(x2, fg_ref[...]) if final else x2


def _combine(x1, st, o_exp, mod, final_g, row0, n_req, n, slot_blk0, per_req_mod, final):
    d = x1.shape[1]
    n_exp = o_exp.shape[0]
    ecap = st.shape[1]
    cap = ecap // n_exp
    tr = min(n, NORM_ROWS)
    e_chunk = max(1, min(n_exp, ROW_TILE // cap))
    n_k = n_exp // e_chunk
    rb0 = row0 // tr
    rt = n // tr
    mrows = mod.shape[0]
    return pl.pallas_call(
        functools.partial(_combine_kernel, n_k=n_k, per_req_mod=per_req_mod, final=final),
        grid=(n_req, rt, n_k),
        in_specs=[pl.BlockSpec((tr, d), lambda b, t, k: (rb0 + b * rt + t, 0)),
                  pl.BlockSpec((tr, e_chunk * cap), lambda b, t, k: (b * rt + t, k)),
                  pl.BlockSpec((e_chunk, cap, d), lambda b, t, k: (k, slot_blk0 + b, 0)),
                  pl.BlockSpec((mrows, d), lambda b, t, k: (0, 3)),
                  pl.BlockSpec((1, d), lambda b, t, k: (0, 0))],
        out_specs=pl.BlockSpec((tr, d), lambda b, t, k: (b * rt + t, 0)),
        out_shape=jax.ShapeDtypeStruct((n_req * n, d), F32),
        compiler_params=_params("arbitrary", "arbitrary", "arbitrary"),
        name="combine_n%d" % n,
    )(x1, st, o_exp, mod, final_g.reshape(1, d))


def kernel(x_prompt, x_sample, cache_k, cache_v, c, c_ctx, w_ada, b_ada, norm1_g, norm2_g, w_in, conv_w, rpb, w_br_att, w_br_conv, w_out, w_router, w_gate, w_up, w_down, final_g):
    n_req, seq, d = x_prompt.shape
    n_dec, dec_seq, _ = x_sample.shape
    depth = w_in.shape[0]
    n_heads, hd = cache_k.shape[-2:]
    a_dim = n_heads * hd
    c_dim = conv_w.shape[-1]
    n_exp = w_router.shape[-1]
    rp, rs = n_req * seq, n_dec * dec_seq
    tm = ROW_TILE
    assert rp % tm == 0 and rs % tm == 0 and dec_seq % NORM_ROWS == 0 and seq % NORM_ROWS == 0

    xp = x_prompt.reshape(rp, d)
    xs = x_sample.reshape(rs, d)
    ck = cache_k.reshape(cache_k.shape[:3] + (a_dim,))
    cv = cache_v.reshape(cache_v.shape[:3] + (a_dim,))
    cond = jnp.concatenate([c_ctx[None, :], c], axis=0)
    cond = jnp.pad(cond, ((0, -cond.shape[0] % SUBLANES), (0, 0)))
    w_router_pad = jnp.pad(w_router, ((0, 0), (0, 0), (0, -n_exp % LANES)))
    cap_p = CAPACITY_FACTOR * seq // n_exp
    cap_s = CAPACITY_FACTOR * dec_seq // n_exp
    slots = n_req * cap_p + n_dec * cap_s

    new_k, new_v = [], []
    for l in range(depth):
        last = l == depth - 1
        mod1 = _adaln(cond, w_ada, b_ada, l, 2 * d)
        h = _norm1(xp, xs, mod1, norm1_g, l, dec_seq)
        q = _proj(h, w_in, l, 0, a_dim, BF16, "proj_q")
        k_p, k_s = _proj_split(h, w_in, l, a_dim, a_dim, rp, "proj_k")
        v_p, v_s = _proj_split(h, w_in, l, 2 * a_dim, a_dim, rp, "proj_v")
        y_conv = _conv_proj(h, w_in, conv_w, l, 3 * a_dim, c_dim, rp, seq, dec_seq)
        gates, mod = _proj_ada(h, w_in, l, 3 * a_dim + 3 * c_dim, 2 * d, cond.T, 1 + n_dec,
                               w_ada, b_ada, 2 * d, "proj_gates")
        new_k.append(k_p.reshape(n_req, seq, n_heads, hd))
        new_v.append(v_p.reshape(n_req, seq, n_heads, hd))

        y_att_p = _ctx_attention(q, k_p, v_p, n_req, seq, n_heads, hd)
        y_att_s = _nbr_attention(rpb, q, k_s, v_s, ck, cv, l, n_heads, hd)
        merged = _merge(y_att_p, y_att_s, y_conv, gates, w_br_att, w_br_conv, l, 0, d)
        x1 = _outproj(merged, w_out, xp, xs, mod, l, dec_seq)

        h2, logits = _norm2(x1, mod, norm2_g, w_router_pad, l, rp, dec_seq)
        xg, st_p, g = _route(logits, h2, 0, n_req, seq, n_exp, slots, None)
        xg, st_s, g = _route(logits, h2, rp, n_dec, dec_seq, n_exp, slots,
                             (xg, g, n_req * cap_p // cap_s))
        o_exp = _ffn_down(_ffn_up(xg, w_gate, w_up, l), w_down, g, l)
        xp = _combine(x1, st_p, o_exp, mod, final_g, 0, n_req, seq, 0, False, last)
        xs = _combine(x1, st_s, o_exp, mod, final_g, rp, n_dec, dec_seq,
                      n_req * cap_p // cap_s, True, last)

    y_prompt = xp.reshape(n_req, seq, d)
    y_sample = xs.reshape(n_dec, dec_seq, d)
    return (y_prompt, y_sample, jnp.stack(new_k, axis=1), jnp.stack(new_v, axis=1))
```
